```python
import jax
import jax.numpy as jnp
from jax import lax
import numpy as np
import math

D_MODEL = 1024
BATCH = 4
SEQ = 8192
DEPTH = 2

HEAD_DIM = 64
ROT_DIM = HEAD_DIM // 4
ROPE_THETA = 500000.0
NORM_EPS = 1e-6
MIX_WIDTH = D_MODEL

S5_WIDTH = MIX_WIDTH // 2
S5_GROUP = 16
S5_GROUPS = S5_WIDTH // S5_GROUP
S5_STATE = 64
MOBA_WIDTH = MIX_WIDTH - S5_WIDTH
MOBA_HEADS = MOBA_WIDTH // HEAD_DIM
MOBA_BLOCK = 256
MOBA_TOPK = 3
MOBA_QCHUNK = 32
EVEN_SPLITS = (S5_WIDTH, MOBA_WIDTH, MOBA_WIDTH, MOBA_WIDTH)
EVEN_PROJ = sum(EVEN_SPLITS)

RWKV_WIDTH = MIX_WIDTH // 2
RWKV_HEADS = RWKV_WIDTH // HEAD_DIM
RWKV_W_LORA = 64
RWKV_A_LORA = 64
RWKV_G_LORA = 128
RWKV_GN_EPS = 64e-5
RWKV_SPLITS = (RWKV_WIDTH, RWKV_WIDTH, RWKV_WIDTH, RWKV_W_LORA, RWKV_A_LORA, RWKV_G_LORA)
RWKV_PROJ = sum(RWKV_SPLITS)
DSA_WIDTH = MIX_WIDTH - RWKV_WIDTH
DSA_HEADS = DSA_WIDTH // HEAD_DIM
DSA_IDX_HEADS = 4
DSA_IDX_DIM = HEAD_DIM
DSA_TOPK = 256
DSA_QBLOCK = 128
DSA_SPLITS = (DSA_WIDTH, HEAD_DIM, HEAD_DIM, DSA_IDX_HEADS * DSA_IDX_DIM, DSA_IDX_DIM, DSA_IDX_HEADS)
ODD_PROJ = RWKV_PROJ + sum(DSA_SPLITS)

FFN_DENSE = 2816
N_EXPERTS = 8
TOP_K = 2
FFN_EXPERT = 3584
MOE_BLOCK = 256

N_EVEN = (DEPTH + 1) // 2
N_ODD = DEPTH // 2

kernel_name = 'hybrid_s5_moba_rwkv7_dsa_moe'


def split_cols(t, sizes):
    return jnp.split(t, [int(c) for c in np.cumsum(sizes)[:-1]], axis=-1)


def rms_norm(x, g):
    xf = x.astype(jnp.float32)
    y = xf * lax.rsqrt(jnp.mean(xf * xf, axis=-1, keepdims=True) + NORM_EPS)
    return (y * g.astype(jnp.float32)).astype(x.dtype)


def partial_rope(x, pos):
    half = ROT_DIM // 2
    inv_freq = ROPE_THETA ** (-jnp.arange(half, dtype=jnp.float32) / half)
    ang = pos.astype(jnp.float32)[:, None] * inv_freq[None, :]
    cos = jnp.cos(ang)[None, :, None, :]
    sin = jnp.sin(ang)[None, :, None, :]
    x1 = x[..., :half].astype(jnp.float32)
    x2 = x[..., half:ROT_DIM].astype(jnp.float32)
    rot = jnp.concatenate([x1 * cos - x2 * sin, x2 * cos + x1 * sin], axis=-1).astype(x.dtype)
    return jnp.concatenate([rot, x[..., ROT_DIM:]], axis=-1)


def swiglu(x, w_gate, w_up, w_down):
    return (jax.nn.silu(x @ w_gate) * (x @ w_up)) @ w_down


def s5_mixer(u, a_re, a_im, log_dt, b_re, b_im, c_re, c_im, d, glu_w, glu_b):
    f32 = jnp.float32
    Bsz, S, _ = u.shape
    uf = u.astype(f32)
    ug = uf.reshape(Bsz, S, S5_GROUPS, S5_GROUP)
    ar, ai = a_re.astype(f32), a_im.astype(f32)
    dt = jnp.exp(log_dt.astype(f32))[:, None]
    mag = jnp.exp(dt * ar)
    abar_re = mag * jnp.cos(dt * ai)
    abar_im = mag * jnp.sin(dt * ai)
    den = ar * ar + ai * ai
    xr, xi = abar_re - 1.0, abar_im
    f_re = (xr * ar + xi * ai) / den
    f_im = (xi * ar - xr * ai) / den
    br, bi = b_re.astype(f32), b_im.astype(f32)
    bbar_re = f_re[..., None] * br - f_im[..., None] * bi
    bbar_im = f_re[..., None] * bi + f_im[..., None] * br
    bu_re = jnp.einsum('gpc,bsgc->bsgp', bbar_re, ug)
    bu_im = jnp.einsum('gpc,bsgc->bsgp', bbar_im, ug)
    a_re_t = jnp.broadcast_to(abar_re, bu_re.shape)
    a_im_t = jnp.broadcast_to(abar_im, bu_re.shape)

    def combine(e1, e2):
        a1r, a1i, b1r, b1i = e1
        a2r, a2i, b2r, b2i = e2
        return (a1r * a2r - a1i * a2i, a1r * a2i + a1i * a2r,
                a2r * b1r - a2i * b1i + b2r, a2r * b1i + a2i * b1r + b2i)

    _, _, st_re, st_im = lax.associative_scan(combine, (a_re_t, a_im_t, bu_re, bu_im), axis=1)
    y = (jnp.einsum('gcp,bsgp->bsgc', c_re.astype(f32), st_re)
         - jnp.einsum('gcp,bsgp->bsgc', c_im.astype(f32), st_im))
    y = y.reshape(Bsz, S, S5_WIDTH) + d.astype(f32) * uf
    y = jax.nn.gelu(y)
    y = y * jax.nn.sigmoid(y @ glu_w.astype(f32) + glu_b.astype(f32))
    return y.astype(u.dtype)


def moba_attention(q, k, v):
    f32 = jnp.float32
    Bsz, S, H, hd = q.shape
    nb = -(-S // MOBA_BLOCK)
    pad = nb * MOBA_BLOCK - S
    kp = jnp.pad(k, ((0, 0), (0, pad), (0, 0), (0, 0)))
    vp = jnp.pad(v, ((0, 0), (0, pad), (0, 0), (0, 0)))
    kb = kp.reshape(Bsz, nb, MOBA_BLOCK, H, hd).transpose(0, 3, 1, 2, 4)
    vb = vp.reshape(Bsz, nb, MOBA_BLOCK, H, hd).transpose(0, 3, 1, 2, 4)
    qh = q.transpose(0, 2, 1, 3)
    pos = jnp.arange(S, dtype=jnp.int32)
    qblk = pos // MOBA_BLOCK
    own = jnp.broadcast_to(qblk[None, None, :, None], (Bsz, H, S, 1))
    n_sel = min(MOBA_TOPK, nb - 1)
    if n_sel > 0:
        k_mean = jnp.mean(kb.astype(f32), axis=3)
        gate = jnp.einsum('bhsd,bhnd->bhsn', qh.astype(f32), k_mean)
        past = jnp.arange(nb)[None, :] < qblk[:, None]
        gate = jnp.where(past[None, None], gate, -jnp.inf)
        _, top_idx = lax.top_k(gate, n_sel)
        idx = jnp.concatenate([top_idx.astype(jnp.int32), own], axis=-1)
        valid = jnp.concatenate([jnp.arange(n_sel)[None, :] < qblk[:, None],
                                 jnp.ones((S, 1), bool)], axis=-1)
    else:
        idx = own
        valid = jnp.ones((S, 1), bool)
    n_j = idx.shape[-1]
    nc = S // MOBA_QCHUNK
    scale = hd ** -0.5

    def to_chunks(t):
        return jnp.moveaxis(t.reshape(Bsz, H, nc, MOBA_QCHUNK, *t.shape[3:]), 2, 0)

    gather = jax.vmap(jax.vmap(lambda blocks, i: blocks[i]))

    def chunk_attn(args):
        qc, ic, vc, pc = args
        kg = gather(kb, ic)
        vg = gather(vb, ic)
        s = jnp.einsum('bhqd,bhqjkd->bhqjk', qc, kg, preferred_element_type=f32) * scale
        kpos = ic[..., None] * MOBA_BLOCK + jnp.arange(MOBA_BLOCK)
        mask = (kpos <= pc[None, None, :, None, None]) & vc[None, None, :, :, None]
        s = jnp.where(mask, s, -jnp.inf)
        p = jax.nn.softmax(s.reshape(Bsz, H, MOBA_QCHUNK, n_j * MOBA_BLOCK), axis=-1)
        p = p.reshape(s.shape)
        return jnp.einsum('bhqjk,bhqjkd->bhqd', p, vg.astype(f32)).astype(q.dtype)

    out = lax.map(chunk_attn, (to_chunks(qh), to_chunks(idx),
                               valid.reshape(nc, MOBA_QCHUNK, n_j), pos.reshape(nc, MOBA_QCHUNK)))
    out = jnp.moveaxis(out, 0, 2).reshape(Bsz, H, S, hd)
    return out.transpose(0, 2, 1, 3)


def rwkv7_time_mix(p, mu, w0, w_up, a0, a_up, g_up, k_k, k_a, r_k, ln_w, ln_b):
    f32 = jnp.float32
    out_dtype = p.dtype
    Bsz, S, _ = p.shape
    p = p.astype(f32)
    prev = jnp.pad(p, ((0, 0), (1, 0), (0, 0)))[:, :S]
    p = p + (prev - p) * mu.astype(f32)
    r, k, v, wd, ad, gd = split_cols(p, RWKV_SPLITS)
    w_log = -jax.nn.softplus(-(w0.astype(f32) + jnp.tanh(wd) @ w_up.astype(f32))) - 0.5
    decay = jnp.exp(-jnp.exp(w_log))
    a = jax.nn.sigmoid(a0.astype(f32) + ad @ a_up.astype(f32))
    g = jax.nn.sigmoid(gd) @ g_up.astype(f32)

    def heads(t):
        return t.reshape(Bsz, S, RWKV_HEADS, HEAD_DIM)

    kk = heads(k * k_k.astype(f32))
    kk = kk / jnp.maximum(jnp.sqrt(jnp.sum(kk * kk, axis=-1, keepdims=True)), 1e-12)
    k = k * (1.0 + (a - 1.0) * k_a.astype(f32))
    r, decay, k, v, a = heads(r), heads(decay), heads(k), heads(v), heads(a)
    xs = (jnp.moveaxis(r, 1, 0), jnp.moveaxis(decay, 1, 0), jnp.moveaxis(k, 1, 0),
          jnp.moveaxis(v, 1, 0), jnp.moveaxis(-kk, 1, 0), jnp.moveaxis(kk * a, 1, 0))

    def step(state, inp):
        r_t, w_t, k_t, v_t, a_t, b_t = inp
        sa = jnp.einsum('bhvk,bhk->bhv', state, a_t)
        state = (state * w_t[:, :, None, :] + sa[..., None] * b_t[:, :, None, :]
                 + v_t[..., None] * k_t[:, :, None, :])
        return state, jnp.einsum('bhvk,bhk->bhv', state, r_t)

    state0 = jnp.zeros((Bsz, RWKV_HEADS, HEAD_DIM, HEAD_DIM), f32)
    _, y = lax.scan(step, state0, xs)
    y = jnp.moveaxis(y, 0, 1)
    mean = jnp.mean(y, axis=-1, keepdims=True)
    var = jnp.mean(jnp.square(y - mean), axis=-1, keepdims=True)
    y = ((y - mean) * lax.rsqrt(var + RWKV_GN_EPS)).reshape(Bsz, S, RWKV_WIDTH)
    y = y * ln_w.astype(f32) + ln_b.astype(f32)
    bonus = jnp.sum(r * k * r_k.astype(f32), axis=-1, keepdims=True) * v
    return ((y + bonus.reshape(Bsz, S, RWKV_WIDTH)) * g).astype(out_dtype)


def dsa_attention(q, k, v, q_idx, k_idx, w_idx):
    f32 = jnp.float32
    Bsz, S, H, hd = q.shape
    topk = min(DSA_TOPK, S // 4)
    nq = S // DSA_QBLOCK
    scale = hd ** -0.5
    w_scaled = w_idx.astype(f32) * (DSA_IDX_HEADS ** -0.5 * DSA_IDX_DIM ** -0.5)
    pos = jnp.arange(S, dtype=jnp.int32)

    def blocks(t):
        return jnp.moveaxis(t.reshape(Bsz, nq, DSA_QBLOCK, *t.shape[2:]), 1, 0)

    gather = jax.vmap(lambda t, i: t[i])

    def block_attn(args):
        qb, qib, wb, pb = args
        logits = jnp.einsum('bqhd,bsd->bqhs', qib, k_idx, preferred_element_type=f32)
        score = jnp.einsum('bqhs,bqh->bqs', jax.nn.relu(logits), wb)
        causal = pos[None, :] <= pb[:, None]
        score = jnp.where(causal[None], score, -jnp.inf)
        _, sel = lax.top_k(score, topk)
        kg = gather(k, sel)
        vg = gather(v, sel)
        s = jnp.einsum('bqhd,bqkd->bqhk', qb, kg, preferred_element_type=f32) * scale
        ok = (sel <= pb[None, :, None])[:, :, None, :]
        p = jax.nn.softmax(jnp.where(ok, s, -jnp.inf), axis=-1)
        return jnp.einsum('bqhk,bqkd->bqhd', p, vg.astype(f32)).astype(q.dtype)

    out = lax.map(block_attn, (blocks(q), blocks(q_idx), blocks(w_scaled),
                               pos.reshape(nq, DSA_QBLOCK)))
    return jnp.moveaxis(out, 0, 1).reshape(Bsz, S, H * hd)


def moe_swiglu(x, router, w_gate, w_up, w_down):
    f32 = jnp.float32
    Bsz, S, D = x.shape
    T = Bsz * S
    xt = x.reshape(T, D)
    logits = xt.astype(f32) @ router.astype(f32)
    top_val, top_exp = lax.top_k(logits, TOP_K)
    gates = jax.nn.softmax(top_val, axis=-1)
    flat_exp = top_exp.reshape(-1).astype(jnp.int32)
    flat_tok = jnp.repeat(jnp.arange(T, dtype=jnp.int32), TOP_K)
    flat_gate = gates.reshape(-1)
    order = jnp.argsort(flat_exp)
    sorted_exp = flat_exp[order]
    counts = jnp.bincount(flat_exp, length=N_EXPERTS).astype(jnp.int32)
    padded = (counts + MOE_BLOCK - 1) // MOE_BLOCK * MOE_BLOCK
    pad_end = jnp.cumsum(padded)
    pad_start = pad_end - padded
    grp_start = jnp.cumsum(counts) - counts
    rank = jnp.arange(T * TOP_K, dtype=jnp.int32) - grp_start[sorted_exp]
    dest = pad_start[sorted_exp] + rank
    n_blocks = -(-(T * TOP_K + N_EXPERTS * (MOE_BLOCK - 1)) // MOE_BLOCK)
    n_slots = n_blocks * MOE_BLOCK
    slot_tok = jnp.full((n_slots,), T, jnp.int32).at[dest].set(flat_tok[order])
    slot_gate = jnp.zeros((n_slots,), f32).at[dest].set(flat_gate[order])
    block_start = jnp.arange(n_blocks, dtype=jnp.int32) * MOE_BLOCK
    block_exp = jnp.minimum(jnp.searchsorted(pad_end, block_start, side='right'), N_EXPERTS - 1)
    x_pad = jnp.concatenate([xt, jnp.zeros((1, D), xt.dtype)], axis=0)

    def run_block(args):
        tok, e = args
        xb = x_pad[tok]
        return swiglu(xb, w_gate[e], w_up[e], w_down[e])

    yb = lax.map(run_block, (slot_tok.reshape(n_blocks, MOE_BLOCK), block_exp))
    y = jnp.zeros((T + 1, D), f32).at[slot_tok].add(
        yb.reshape(n_slots, D).astype(f32) * slot_gate[:, None])
    return y[:T].reshape(Bsz, S, D).astype(x.dtype)


def even_layer(h, pos, norm_mix, w_in, a_re, a_im, log_dt, b_re, b_im, c_re, c_im, d,
               glu_w, glu_b, w_out, norm_ffn, w_gate, w_up, w_down):
    Bsz, S, _ = h.shape
    proj = rms_norm(h, norm_mix) @ w_in
    u, q, k, v = split_cols(proj, EVEN_SPLITS)
    y_a = s5_mixer(u, a_re, a_im, log_dt, b_re, b_im, c_re, c_im, d, glu_w, glu_b)

    def heads(t):
        return t.reshape(Bsz, S, MOBA_HEADS, HEAD_DIM)

    y_b = moba_attention(partial_rope(heads(q), pos), partial_rope(heads(k), pos), heads(v))
    h = h + jnp.concatenate([y_a, y_b.reshape(Bsz, S, MOBA_WIDTH)], axis=-1) @ w_out
    return h + swiglu(rms_norm(h, norm_ffn), w_gate, w_up, w_down)


def odd_layer(h, pos, norm_mix, w_in, mu, w0, w_up_w, a0, a_up, g_up, k_k, k_a, r_k, ln_w, ln_b,
              w_out, norm_ffn, router, e_gate, e_up, e_down):
    Bsz, S, _ = h.shape
    proj = rms_norm(h, norm_mix) @ w_in
    proj_c, proj_d = jnp.split(proj, [RWKV_PROJ], axis=-1)
    y_c = rwkv7_time_mix(proj_c, mu, w0, w_up_w, a0, a_up, g_up, k_k, k_a, r_k, ln_w, ln_b)
    q, k, v, qi, ki, wi = split_cols(proj_d, DSA_SPLITS)
    q = partial_rope(q.reshape(Bsz, S, DSA_HEADS, HEAD_DIM), pos)
    k = partial_rope(k[:, :, None, :], pos)[:, :, 0]
    qi = partial_rope(qi.reshape(Bsz, S, DSA_IDX_HEADS, DSA_IDX_DIM), pos)
    ki = partial_rope(ki[:, :, None, :], pos)[:, :, 0]
    y_d = dsa_attention(q, k, v, qi, ki, wi)
    h = h + jnp.concatenate([y_c, y_d], axis=-1) @ w_out
    return h + moe_swiglu(rms_norm(h, norm_ffn), router, e_gate, e_up, e_down)


def setup_inputs(seed: int = 0) -> dict:
    key = jax.random.key(seed)
    ks = iter(jax.random.split(key, 48))
    f32 = jnp.float32

    def nrm(shape, scale):
        return scale * jax.random.normal(next(ks), shape, f32)

    D = D_MODEL
    NE, NO = N_EVEN, N_ODD
    n = jnp.arange(S5_STATE, dtype=f32)
    inp = {}
    inp['x'] = nrm((BATCH, SEQ, D), 1.0)
    inp['e_norm_mix'] = 1.0 + nrm((NE, D), 0.02)
    inp['e_w_in'] = nrm((NE, D, EVEN_PROJ), D ** -0.5)
    inp['s5_a_re'] = -0.5 + nrm((NE, S5_GROUPS, S5_STATE), 0.01)
    inp['s5_a_im'] = math.pi * n + nrm((NE, S5_GROUPS, S5_STATE), 0.01)
    inp['s5_log_dt'] = jax.random.uniform(next(ks), (NE, S5_GROUPS), f32,
                                          minval=math.log(1e-3), maxval=math.log(1e-1))
    inp['s5_b_re'] = nrm((NE, S5_GROUPS, S5_STATE, S5_GROUP), (2 * S5_GROUP) ** -0.5)
    inp['s5_b_im'] = nrm((NE, S5_GROUPS, S5_STATE, S5_GROUP), (2 * S5_GROUP) ** -0.5)
    inp['s5_c_re'] = nrm((NE, S5_GROUPS, S5_GROUP, S5_STATE), (2 * S5_STATE) ** -0.5)
    inp['s5_c_im'] = nrm((NE, S5_GROUPS, S5_GROUP, S5_STATE), (2 * S5_STATE) ** -0.5)
    inp['s5_d'] = nrm((NE, S5_WIDTH), 1.0)
    inp['s5_glu_w'] = nrm((NE, S5_WIDTH, S5_WIDTH), S5_WIDTH ** -0.5)
    inp['s5_glu_b'] = nrm((NE, S5_WIDTH), 0.01)
    inp['e_w_out'] = nrm((NE, MIX_WIDTH, D), MIX_WIDTH ** -0.5)
    inp['e_norm_ffn'] = 1.0 + nrm((NE, D), 0.02)
    inp['ffn_w_gate'] = nrm((NE, D, FFN_DENSE), D ** -0.5)
    inp['ffn_w_up'] = nrm((NE, D, FFN_DENSE), D ** -0.5)
    inp['ffn_w_down'] = nrm((NE, FFN_DENSE, D), FFN_DENSE ** -0.5)
    inp['o_norm_mix'] = 1.0 + nrm((NO, D), 0.02)
    inp['o_w_in'] = nrm((NO, D, ODD_PROJ), D ** -0.5)
    inp['rwkv_mu'] = jax.random.uniform(next(ks), (NO, RWKV_PROJ), f32)
    inp['rwkv_w0'] = -2.0 + nrm((NO, RWKV_WIDTH), 1.0)
    inp['rwkv_w_up'] = nrm((NO, RWKV_W_LORA, RWKV_WIDTH), 0.1)
    inp['rwkv_a0'] = nrm((NO, RWKV_WIDTH), 0.5)
    inp['rwkv_a_up'] = nrm((NO, RWKV_A_LORA, RWKV_WIDTH), 0.1)
    inp['rwkv_g_up'] = nrm((NO, RWKV_G_LORA, RWKV_WIDTH), RWKV_G_LORA ** -0.5)
    inp['rwkv_k_k'] = 0.85 + nrm((NO, RWKV_WIDTH), 0.02)
    inp['rwkv_k_a'] = 1.0 + nrm((NO, RWKV_WIDTH), 0.02)
    inp['rwkv_r_k'] = nrm((NO, RWKV_HEADS, HEAD_DIM), 0.1)
    inp['rwkv_ln_w'] = 1.0 + nrm((NO, RWKV_WIDTH), 0.02)
    inp['rwkv_ln_b'] = nrm((NO, RWKV_WIDTH), 0.01)
    inp['o_w_out'] = nrm((NO, MIX_WIDTH, D), MIX_WIDTH ** -0.5)
    inp['o_norm_ffn'] = 1.0 + nrm((NO, D), 0.02)
    inp['moe_router'] = nrm((NO, D, N_EXPERTS), D ** -0.5)
    inp['moe_w_gate'] = nrm((NO, N_EXPERTS, D, FFN_EXPERT), D ** -0.5)
    inp['moe_w_up'] = nrm((NO, N_EXPERTS, D, FFN_EXPERT), D ** -0.5)
    inp['moe_w_down'] = nrm((NO, N_EXPERTS, FFN_EXPERT, D), FFN_EXPERT ** -0.5)
    inp['final_norm'] = 1.0 + nrm((D,), 0.02)
    return inp


def reference(x, e_norm_mix, e_w_in, s5_a_re, s5_a_im, s5_log_dt, s5_b_re, s5_b_im, s5_c_re,
              s5_c_im, s5_d, s5_glu_w, s5_glu_b, e_w_out, e_norm_ffn, ffn_w_gate, ffn_w_up,
              ffn_w_down, o_norm_mix, o_w_in, rwkv_mu, rwkv_w0, rwkv_w_up, rwkv_a0, rwkv_a_up,
              rwkv_g_up, rwkv_k_k, rwkv_k_a, rwkv_r_k, rwkv_ln_w, rwkv_ln_b, o_w_out, o_norm_ffn,
              moe_router, moe_w_gate, moe_w_up, moe_w_down, final_norm):
    pos = jnp.arange(x.shape[1], dtype=jnp.int32)
    h = x
    for layer in range(DEPTH):
        j = layer // 2
        if layer % 2 == 0:
            h = even_layer(h, pos, e_norm_mix[j], e_w_in[j], s5_a_re[j], s5_a_im[j], s5_log_dt[j],
                           s5_b_re[j], s5_b_im[j], s5_c_re[j], s5_c_im[j], s5_d[j], s5_glu_w[j],
                           s5_glu_b[j], e_w_out[j], e_norm_ffn[j], ffn_w_gate[j], ffn_w_up[j],
                           ffn_w_down[j])
        else:
            h = odd_layer(h, pos, o_norm_mix[j], o_w_in[j], rwkv_mu[j], rwkv_w0[j], rwkv_w_up[j],
                          rwkv_a0[j], rwkv_a_up[j], rwkv_g_up[j], rwkv_k_k[j], rwkv_k_a[j],
                          rwkv_r_k[j], rwkv_ln_w[j], rwkv_ln_b[j], o_w_out[j], o_norm_ffn[j],
                          moe_router[j], moe_w_gate[j], moe_w_up[j], moe_w_down[j])
    return rms_norm(h, final_norm)
```

```python
import functools
import math

import jax
import jax.numpy as jnp
import numpy as np
from jax import lax
from jax.experimental import pallas as pl
from jax.experimental.pallas import tpu as pltpu

F32 = jnp.float32
BF16 = jnp.bfloat16
HIGHEST = lax.Precision.HIGHEST

LANES = 128
SUBLANES = 8
VMEM_LIMIT_BYTES = 56 * 1024 * 1024

D_MODEL = 1024
HEAD_DIM = 64
ROT_DIM = HEAD_DIM // 4
ROPE_THETA = 500000.0
NORM_EPS = 1e-6
S5_WIDTH = 512
S5_GROUP = 16
S5_GROUPS = S5_WIDTH // S5_GROUP
S5_STATE = 64
S5_NSTATE = S5_GROUPS * S5_STATE
MOBA_WIDTH = 512
MOBA_HEADS = MOBA_WIDTH // HEAD_DIM
MOBA_BLOCK = 256
MOBA_TOPK = 3
RWKV_WIDTH = 512
RWKV_HEADS = RWKV_WIDTH // HEAD_DIM
RWKV_W_LORA = 64
RWKV_A_LORA = 64
RWKV_G_LORA = 128
RWKV_GN_EPS = 64e-5
RWKV_CHUNK = 64
DSA_WIDTH = 512
DSA_HEADS = DSA_WIDTH // HEAD_DIM
DSA_IDX_HEADS = 4
DSA_TOPK = 256
DSA_QBLOCK = 128
N_EXPERTS = 8
NEG_BIG = -1e30


def _cparams(sem):
    return pltpu.CompilerParams(dimension_semantics=sem, vmem_limit_bytes=VMEM_LIMIT_BYTES)


def _rms(x, g):
    return x * lax.rsqrt(jnp.mean(x * x, axis=-1, keepdims=True) + NORM_EPS) * g


def _norm_matmul_kernel(flags_ref, h_ref, g_ref, w_ref, cos_ref, sina_ref, sinb_ref, colf_ref,
                        o_ref, xn_ref):
    j = pl.program_id(1)

    @pl.when(j == 0)
    def _():
        xn_ref[...] = _rms(h_ref[...], g_ref[...]).astype(BF16)

    acc = jnp.dot(xn_ref[...], w_ref[...], preferred_element_type=F32)
    tn = acc.shape[1]

    @pl.when(flags_ref[j] == 0)
    def _():
        o_ref[...] = acc

    @pl.when(flags_ref[j] != 0)
    def _():
        reps = tn // LANES
        colf = colf_ref[...]
        cos = 1.0 + colf * (jnp.tile(cos_ref[...], (1, reps)) - 1.0)
        sina = colf * jnp.tile(sina_ref[...], (1, reps))
        sinb = colf * jnp.tile(sinb_ref[...], (1, reps))
        nxt = pltpu.roll(acc, tn - ROT_DIM // 2, axis=1)
        prv = pltpu.roll(acc, ROT_DIM // 2, axis=1)
        o_ref[...] = acc * cos + nxt * sina + prv * sinb


def _rope_tables(seq):
    half = ROT_DIM // 2
    inv_freq = ROPE_THETA ** (-jnp.arange(half, dtype=F32) / half)
    ang = jnp.arange(seq, dtype=F32)[:, None] * inv_freq[None, :]
    cos, sin = jnp.cos(ang), jnp.sin(ang)
    ones = jnp.ones((seq, HEAD_DIM - ROT_DIM), F32)
    zeros = jnp.zeros((seq, HEAD_DIM - ROT_DIM), F32)
    zh = jnp.zeros((seq, half), F32)
    cos64 = jnp.concatenate([cos, cos, ones], axis=1)
    sina64 = jnp.concatenate([-sin, zh, zeros], axis=1)
    sinb64 = jnp.concatenate([zh, sin, zeros], axis=1)
    rep = LANES // HEAD_DIM
    return jnp.tile(cos64, (1, rep)), jnp.tile(sina64, (1, rep)), jnp.tile(sinb64, (1, rep))


def norm_matmul_rope(h2d, g, w_bf16, rope_cols, seq, *, tm, tn):
    t, d = h2d.shape
    n = w_bf16.shape[1]
    assert t % tm == 0 and n % tn == 0 and seq % tm == 0 and tn % LANES == 0
    nseq = seq // tm
    cos, sina, sinb = _rope_tables(seq)
    tile_flags = jnp.asarray(rope_cols.reshape(n // tn, tn).any(axis=1).astype(np.int32))
    colf = jnp.asarray(rope_cols.astype(np.float32))[None, :]
    grid_spec = pltpu.PrefetchScalarGridSpec(
        num_scalar_prefetch=1,
        grid=(t // tm, n // tn),
        in_specs=[
            pl.BlockSpec((tm, d), lambda i, j, f: (i, 0)),
            pl.BlockSpec((1, d), lambda i, j, f: (0, 0)),
            pl.BlockSpec((d, tn), lambda i, j, f: (0, j)),
            pl.BlockSpec((tm, LANES), lambda i, j, f: (i % nseq, 0)),
            pl.BlockSpec((tm, LANES), lambda i, j, f: (i % nseq, 0)),
            pl.BlockSpec((tm, LANES), lambda i, j, f: (i % nseq, 0)),
            pl.BlockSpec((1, tn), lambda i, j, f: (0, j)),
        ],
        out_specs=pl.BlockSpec((tm, tn), lambda i, j, f: (i, j)),
        scratch_shapes=[pltpu.VMEM((tm, d), BF16)],
    )
    return pl.pallas_call(
        _norm_matmul_kernel,
        grid_spec=grid_spec,
        out_shape=jax.ShapeDtypeStruct((t, n), F32),
        compiler_params=_cparams(("parallel", "arbitrary")),
        name="norm_matmul_rope",
    )(tile_flags, h2d, g.reshape(1, d), w_bf16, cos, sina, sinb, colf)


def _out_proj_kernel(res_ref, a_ref, b_ref, wa_ref, wb_ref, o_ref):
    acc = jnp.dot(a_ref[...].astype(BF16), wa_ref[...], preferred_element_type=F32)
    acc += jnp.dot(b_ref[...].astype(BF16), wb_ref[...], preferred_element_type=F32)
    o_ref[...] = res_ref[...] + acc


def out_proj_residual(res, a, b, w_bf16, *, tm):
    t, d = res.shape
    ka, kb = a.shape[1], b.shape[1]
    wa, wb = w_bf16[:ka], w_bf16[ka:]
    return pl.pallas_call(
        _out_proj_kernel,
        grid=(t // tm,),
        in_specs=[
            pl.BlockSpec((tm, d), lambda i: (i, 0)),
            pl.BlockSpec((tm, ka), lambda i: (i, 0)),
            pl.BlockSpec((tm, kb), lambda i: (i, 0)),
            pl.BlockSpec((ka, d), lambda i: (0, 0)),
            pl.BlockSpec((kb, d), lambda i: (0, 0)),
        ],
        out_specs=pl.BlockSpec((tm, d), lambda i: (i, 0)),
        out_shape=jax.ShapeDtypeStruct((t, d), F32),
        compiler_params=_cparams(("parallel",)),
        name="out_proj_residual",
    )(res, a, b, wa, wb)


def _dense_ffn_kernel(h_ref, g_ref, wg_ref, wu_ref, wd_ref, o_ref, xn_ref, acc_ref):
    f = pl.program_id(1)

    @pl.when(f == 0)
    def _():
        xn_ref[...] = _rms(h_ref[...], g_ref[...]).astype(BF16)
        acc_ref[...] = jnp.zeros_like(acc_ref)

    xn = xn_ref[...]
    gate = jnp.dot(xn, wg_ref[...], preferred_element_type=F32)
    up = jnp.dot(xn, wu_ref[...], preferred_element_type=F32)
    act = (gate * jax.nn.sigmoid(gate) * up).astype(BF16)
    acc_ref[...] += jnp.dot(act, wd_ref[...], preferred_element_type=F32)

    @pl.when(f == pl.num_programs(1) - 1)
    def _():
        o_ref[...] = h_ref[...] + acc_ref[...]


def dense_ffn_residual(h2d, g, wg, wu, wd, *, tm, tf):
    t, d = h2d.shape
    fdim = wg.shape[1]
    assert t % tm == 0 and fdim % tf == 0
    return pl.pallas_call(
        _dense_ffn_kernel,
        grid=(t // tm, fdim // tf),
        in_specs=[
            pl.BlockSpec((tm, d), lambda i, f: (i, 0)),
            pl.BlockSpec((1, d), lambda i, f: (0, 0)),
            pl.BlockSpec((d, tf), lambda i, f: (0, f)),
            pl.BlockSpec((d, tf), lambda i, f: (0, f)),
            pl.BlockSpec((tf, d), lambda i, f: (f, 0)),
        ],
        out_specs=pl.BlockSpec((tm, d), lambda i, f: (i, 0)),
        out_shape=jax.ShapeDtypeStruct((t, d), F32),
        scratch_shapes=[pltpu.VMEM((tm, d), BF16), pltpu.VMEM((tm, d), F32)],
        compiler_params=_cparams(("parallel", "arbitrary")),
        name="dense_ffn_residual",
    )(h2d, g.reshape(1, d), wg, wu, wd)


S5_LANE_CHUNK = 512


def _s5_kernel(u_ref, bre_ref, bim_ref, cre_ref, cim_ref, stepr_ref, stepi_ref, carr_ref, cari_ref,
               d_ref, gw_ref, gb_ref, o_ref, xr_ref, xi_ref, sr_ref, si_ref):
    s = pl.program_id(1)
    ts = u_ref.shape[1]

    @pl.when(s == 0)
    def _():
        sr_ref[...] = jnp.zeros_like(sr_ref)
        si_ref[...] = jnp.zeros_like(si_ref)

    u = u_ref[0]
    ub = u.astype(BF16)
    xr_ref[...] = jnp.dot(ub, bre_ref[...], preferred_element_type=F32)
    xi_ref[...] = jnp.dot(ub, bim_ref[...], preferred_element_type=F32)

    for c in range(S5_NSTATE // S5_LANE_CHUNK):
        lanes = pl.ds(c * S5_LANE_CHUNK, S5_LANE_CHUNK)
        steps = [(stepr_ref[k, :, lanes], stepi_ref[k, :, lanes]) for k in range(3)]
        pr, pi = carr_ref[:, lanes], cari_ref[:, lanes]

        def body(g, carry, lanes=lanes, steps=steps, pr=pr, pi=pi):
            cr, ci = carry
            rows = pl.ds(pl.multiple_of(g * SUBLANES, SUBLANES), SUBLANES)
            xr = xr_ref[rows, lanes]
            xi = xi_ref[rows, lanes]
            for k, (ar, ai) in enumerate(steps):
                rr = pltpu.roll(xr, 1 << k, axis=0)
                ri = pltpu.roll(xi, 1 << k, axis=0)
                xr, xi = xr + ar * rr - ai * ri, xi + ar * ri + ai * rr
            xr = xr + pr * cr - pi * ci
            xi = xi + pr * ci + pi * cr
            xr_ref[rows, lanes] = xr
            xi_ref[rows, lanes] = xi
            return xr[SUBLANES - 1:SUBLANES, :], xi[SUBLANES - 1:SUBLANES, :]

        cr, ci = lax.fori_loop(0, ts // SUBLANES, body, (sr_ref[:, lanes], si_ref[:, lanes]))
        sr_ref[:, lanes] = cr
        si_ref[:, lanes] = ci

    y = jnp.dot(xr_ref[...].astype(BF16), cre_ref[...], preferred_element_type=F32)
    y -= jnp.dot(xi_ref[...].astype(BF16), cim_ref[...], preferred_element_type=F32)
    y = jax.nn.gelu(y + d_ref[...] * u)
    z = jnp.dot(y.astype(BF16), gw_ref[...], preferred_element_type=F32) + gb_ref[...]
    o_ref[0] = y * jax.nn.sigmoid(z)


def _s5_params(a_re, a_im, log_dt, b_re, b_im, c_re, c_im):
    dt = jnp.exp(log_dt)[:, None]

    def apow(n):
        mag = jnp.exp(n * dt * a_re)
        return mag * jnp.cos(n * dt * a_im), mag * jnp.sin(n * dt * a_im)

    abar_re, abar_im = apow(1.0)
    den = a_re * a_re + a_im * a_im
    xr, xi = abar_re - 1.0, abar_im
    f_re = (xr * a_re + xi * a_im) / den
    f_im = (xi * a_re - xr * a_im) / den
    bbar_re = f_re[..., None] * b_re - f_im[..., None] * b_im
    bbar_im = f_re[..., None] * b_im + f_im[..., None] * b_re
    eye = jnp.eye(S5_GROUPS, dtype=F32)

    def blockdiag_in(b):
        return jnp.einsum('gpc,gh->gchp', b, eye).reshape(S5_WIDTH, S5_NSTATE)

    def blockdiag_out(c):
        return jnp.einsum('gcp,gh->gphc', c, eye).reshape(S5_NSTATE, S5_WIDTH)

    row = jnp.arange(SUBLANES)[:, None]
    step_r, step_i = [], []
    for k in range(3):
        pr, pi = apow(float(1 << k))
        keep = row >= (1 << k)
        step_r.append(jnp.where(keep, pr.reshape(1, -1), 0.0))
        step_i.append(jnp.where(keep, pi.reshape(1, -1), 0.0))
    car = [apow(float(r + 1)) for r in range(SUBLANES)]
    car_r = jnp.stack([p[0].reshape(-1) for p in car])
    car_i = jnp.stack([p[1].reshape(-1) for p in car])
    return (blockdiag_in(bbar_re).astype(BF16), blockdiag_in(bbar_im).astype(BF16),
            blockdiag_out(c_re).astype(BF16), blockdiag_out(c_im).astype(BF16),
            jnp.stack(step_r), jnp.stack(step_i), car_r, car_i)


def s5_mixer(proj, a_re, a_im, log_dt, b_re, b_im, c_re, c_im, d, glu_w, glu_b, *, ts):
    bsz, seq, _ = proj.shape
    bre, bim, cre, cim, step_r, step_i, car_r, car_i = _s5_params(a_re, a_im, log_dt, b_re, b_im, c_re, c_im)
    full = lambda shape: pl.BlockSpec(shape, lambda b, s: (0,) * len(shape))
    return pl.pallas_call(
        _s5_kernel,
        grid=(bsz, seq // ts),
        in_specs=[
            pl.BlockSpec((1, ts, S5_WIDTH), lambda b, s: (b, s, 0)),
            full((S5_WIDTH, S5_NSTATE)), full((S5_WIDTH, S5_NSTATE)),
            full((S5_NSTATE, S5_WIDTH)), full((S5_NSTATE, S5_WIDTH)),
            full((3, SUBLANES, S5_NSTATE)), full((3, SUBLANES, S5_NSTATE)),
            full((SUBLANES, S5_NSTATE)), full((SUBLANES, S5_NSTATE)),
            full((1, S5_WIDTH)), full((S5_WIDTH, S5_WIDTH)), full((1, S5_WIDTH)),
        ],
        out_specs=pl.BlockSpec((1, ts, S5_WIDTH), lambda b, s: (b, s, 0)),
        out_shape=jax.ShapeDtypeStruct((bsz, seq, S5_WIDTH), F32),
        scratch_shapes=[pltpu.VMEM((ts, S5_NSTATE), F32), pltpu.VMEM((ts, S5_NSTATE), F32),
                        pltpu.VMEM((1, S5_NSTATE), F32), pltpu.VMEM((1, S5_NSTATE), F32)],
        compiler_params=_cparams(("parallel", "arbitrary")),
        name="s5_mixer",
    )(proj, bre, bim, cre, cim, step_r, step_i, car_r, car_i,
      d.reshape(1, -1), glu_w.astype(BF16), glu_b.reshape(1, -1))


_NT = (((1,), (1,)), ((), ()))
_NN = (((1,), (0,)), ((), ()))
_TN = (((0,), (0,)), ((), ()))


def _split(x):
    hi = x.astype(BF16)
    return hi, (x - hi.astype(F32)).astype(BF16)


def _mm(a, b, dims=_NN):
    return lax.dot_general(a.astype(BF16), b.astype(BF16), dims, preferred_element_type=F32)


def _mm3(a, b, dims=_NN):
    ah, al = _split(a)
    bh, bl = _split(b)
    out = lax.dot_general(ah, bh, dims, preferred_element_type=F32)
    out += lax.dot_general(ah, bl, dims, preferred_element_type=F32)
    out += lax.dot_general(al, bh, dims, preferred_element_type=F32)
    return out


def _mm_exact_rhs(a, b01, dims=_NN):
    ah, al = _split(a)
    out = lax.dot_general(ah, b01, dims, preferred_element_type=F32)
    out += lax.dot_general(al, b01, dims, preferred_element_type=F32)
    return out


def _moba_kernel(q_ref, k_ref, v_ref, o_ref, kmean_ref):
    i = pl.program_id(2)
    blk = MOBA_BLOCK
    seq = k_ref.shape[1]
    nb = seq // blk

    @pl.when(i == 0)
    def _():
        kmean_ref[...] = jnp.mean(k_ref[0].reshape(nb, blk, LANES), axis=1)

    q = q_ref[0]
    lane = lax.broadcasted_iota(jnp.int32, (blk, LANES), 1)
    col = lax.broadcasted_iota(jnp.int32, (blk, nb), 1)
    krow = lax.broadcasted_iota(jnp.int32, (blk, blk), 0)
    kcol = lax.broadcasted_iota(jnp.int32, (blk, blk), 1)
    own_start = pl.multiple_of(i * blk, blk)
    k_own = k_ref[0, pl.ds(own_start, blk), :].astype(BF16)
    v_own = v_ref[0, pl.ds(own_start, blk), :].astype(BF16)
    outs = []
    for hh in range(LANES // HEAD_DIM):
        qh = jnp.where(lane // HEAD_DIM == hh, q, 0.0)
        gate = _mm3(qh, kmean_ref[...], _NT)
        work = jnp.where(col < i, gate, -jnp.inf)
        sel = jnp.zeros((blk, nb), F32)
        for _ in range(MOBA_TOPK):
            m = jnp.max(work, axis=1, keepdims=True)
            first = jnp.min(jnp.where(work == m, col, nb), axis=1, keepdims=True)
            pick = (col == first) & (m > -jnp.inf)
            sel = jnp.where(pick, 1.0, sel)
            work = jnp.where(pick, -jnp.inf, work)

        qb = (qh * (HEAD_DIM ** -0.5)).astype(BF16)
        s = lax.dot_general(qb, k_own, _NT, preferred_element_type=F32)
        s = jnp.where(kcol <= krow, s, NEG_BIG)
        m0 = jnp.max(s, axis=1, keepdims=True)
        p = jnp.exp(s - m0)
        l0 = jnp.sum(p, axis=1, keepdims=True)
        acc0 = jnp.dot(p.astype(BF16), v_own, preferred_element_type=F32)

        def body(j, carry, qb=qb, sel=sel):
            m_prev, l_prev, acc = carry
            start = pl.multiple_of(j * blk, blk)
            kj = k_ref[0, pl.ds(start, blk), :].astype(BF16)
            vj = v_ref[0, pl.ds(start, blk), :].astype(BF16)
            s = lax.dot_general(qb, kj, _NT, preferred_element_type=F32)
            chosen = jnp.max(jnp.where(col == j, sel, 0.0), axis=1, keepdims=True)
            s = jnp.where(chosen > 0.0, s, NEG_BIG)
            m_new = jnp.maximum(m_prev, jnp.max(s, axis=1, keepdims=True))
            alpha = jnp.exp(m_prev - m_new)
            p = jnp.exp(s - m_new)
            l_new = alpha * l_prev + jnp.sum(p, axis=1, keepdims=True)
            acc = alpha * acc + jnp.dot(p.astype(BF16), vj, preferred_element_type=F32)
            return m_new, l_new, acc

        _, l_fin, acc = lax.fori_loop(0, i, body, (m0, l0, acc0))
        outs.append(acc / l_fin)
    o_ref[0] = jnp.where(lane < HEAD_DIM, outs[0], outs[1])


def moba_attention(proj, *, q_col, k_col, v_col):
    bsz, seq, _ = proj.shape
    assert seq % MOBA_BLOCK == 0
    nb = seq // MOBA_BLOCK
    qb, kb, vb = q_col // LANES, k_col // LANES, v_col // LANES
    return pl.pallas_call(
        _moba_kernel,
        grid=(bsz, MOBA_WIDTH // LANES, nb),
        in_specs=[
            pl.BlockSpec((1, MOBA_BLOCK, LANES), lambda b, p, i: (b, i, qb + p)),
            pl.BlockSpec((1, seq, LANES), lambda b, p, i: (b, 0, kb + p)),
            pl.BlockSpec((1, seq, LANES), lambda b, p, i: (b, 0, vb + p)),
        ],
        out_specs=pl.BlockSpec((1, MOBA_BLOCK, LANES), lambda b, p, i: (b, i, p)),
        out_shape=jax.ShapeDtypeStruct((bsz, seq, MOBA_WIDTH), F32),
        scratch_shapes=[pltpu.VMEM((nb, LANES), F32)],
        compiler_params=_cparams(("parallel", "parallel", "arbitrary")),
        name="moba_attention",
    )(proj, proj, proj)


def _head_sum_matrix(width):
    idx = np.arange(width) // HEAD_DIM
    return jnp.asarray((idx[:, None] == idx[None, :]).astype(np.float32)).astype(BF16)


def _rwkv_prep_kernel(p_ref, prev_ref, mu_ref, w0_ref, wup_ref, a0_ref, aup_ref, gup_ref, kk_ref, ka_ref,
                      rk_ref, hsum_ref, r_out, lw_out, k_out, v_out, kk_out, b_out, g_out, bonus_out):
    s = pl.program_id(1)
    p = p_ref[0]
    ts = p.shape[0]
    row = lax.broadcasted_iota(jnp.int32, p.shape, 0)
    last = jnp.where(s == 0, 0.0, 1.0) * prev_ref[0, SUBLANES - 1:SUBLANES, :]
    prev = jnp.where(row == 0, last, pltpu.roll(p, 1, axis=0))
    p = p + (prev - p) * mu_ref[...]
    w = RWKV_WIDTH
    r, k, v = p[:, 0:w], p[:, w:2 * w], p[:, 2 * w:3 * w]
    lora = p[:, 3 * w:3 * w + LANES]
    gd = p[:, 3 * w + LANES:3 * w + 2 * LANES]
    w_log = -jax.nn.softplus(-(w0_ref[...] + _mm3(jnp.tanh(lora), wup_ref[...]))) - 0.5
    lw_out[0] = -jnp.exp(w_log)
    a = jax.nn.sigmoid(a0_ref[...] + _mm3(lora, aup_ref[...]))
    g_out[0] = _mm(jax.nn.sigmoid(gd), gup_ref[...])
    kk = k * kk_ref[...]
    norm = jnp.sqrt(_mm_exact_rhs(kk * kk, hsum_ref[...]))
    kk = kk / jnp.maximum(norm, 1e-12)
    k = k * (1.0 + (a - 1.0) * ka_ref[...])
    r_out[0] = r
    k_out[0] = k
    v_out[0] = v
    kk_out[0] = kk
    b_out[0] = kk * a
    bonus_out[0] = _mm_exact_rhs(r * k * rk_ref[...], hsum_ref[...]) * v


RWKV_GROUP = 256


def _stack_heads(x):
    lane = lax.broadcasted_iota(jnp.int32, x.shape, 1)
    return jnp.concatenate([jnp.where(lane // HEAD_DIM == hh, x, 0.0)
                            for hh in range(RWKV_GROUP // HEAD_DIM)], axis=0)


def _rwkv_scan_kernel(r_ref, lw_ref, k_ref, v_ref, kk_ref, b_ref, g_ref, bonus_ref, lnw_ref, lnb_ref,
                      havg_ref, o_ref, y_ref, st_ref):
    s = pl.program_id(1)
    ts = r_ref.shape[1]
    cl_ = RWKV_CHUNK
    n = RWKV_GROUP
    ngroups = RWKV_WIDTH // RWKV_GROUP

    @pl.when(s == 0)
    def _():
        st_ref[...] = jnp.zeros_like(st_ref)

    ri = lax.broadcasted_iota(jnp.int32, (n, n), 0)
    ci = lax.broadcasted_iota(jnp.int32, (n, n), 1)
    same = (ri // cl_) == (ci // cl_)
    strict = same & ((ci % cl_) < (ri % cl_))
    incl = same & ((ci % cl_) <= (ri % cl_))
    eye = ri == ci
    tr = lax.broadcasted_iota(jnp.int32, (cl_, cl_), 0)
    tc = lax.broadcasted_iota(jnp.int32, (cl_, cl_), 1)
    cum = jnp.where(tc <= tr, 1.0, 0.0).astype(BF16)

    def chunk(c, _):
        rows = pl.ds(pl.multiple_of(c * cl_, cl_), cl_)
        lw = lw_ref[0, rows, :]
        lw_hi, lw_lo = _split(lw)
        cl_in = (jnp.dot(cum, lw_hi, preferred_element_type=F32)
                 + jnp.dot(cum, lw_lo, preferred_element_type=F32))
        cl_ex = cl_in - lw
        tot = cl_in[cl_ - 1:cl_, :]
        e_in = jnp.exp(cl_in)
        e_ninv = jnp.exp(-cl_in)
        e_rem = jnp.exp(tot - cl_in)
        kk = kk_ref[0, rows, :]
        bb = b_ref[0, rows, :]
        k2 = k_ref[0, rows, :]
        a_t = -kk * jnp.exp(cl_ex)
        b_t = bb * e_ninv
        k_t = k2 * e_ninv
        r_t = r_ref[0, rows, :] * e_in
        b_g = bb * e_rem
        k_g = k2 * e_rem
        vv = v_ref[0, rows, :]
        gam = jnp.exp(tot)
        for grp in range(ngroups):
            ls = slice(grp * n, (grp + 1) * n)
            a_s, b_s, k_s, r_s = (_stack_heads(t[:, ls]) for t in (a_t, b_t, k_t, r_t))
            v_s, bg_s, kg_s = (_stack_heads(t[:, ls]) for t in (vv, b_g, k_g))
            gram = _mm3(jnp.concatenate([a_s, r_s], axis=0), jnp.concatenate([b_s, k_s], axis=0), _NT)
            nmat = jnp.where(strict, gram[:n, :n], 0.0)
            m_ak = jnp.where(strict, gram[:n, n:], 0.0)
            m_rb = jnp.where(incl, gram[n:, :n], 0.0)
            m_rk = jnp.where(incl, gram[n:, n:], 0.0)
            winv = jnp.where(eye, 1.0, 0.0) + nmat
            pw = nmat
            for _ in range(int(math.log2(cl_)) - 1):
                pw = _mm3(pw, pw)
                winv = winv + _mm3(winv, pw)
            st = st_ref[grp]
            ar_st = _mm3(jnp.concatenate([a_s, r_s], axis=0), st)
            u_s = _mm3(winv, ar_st[:n] + _mm3(m_ak, v_s))
            y_s = ar_st[n:] + _mm3(m_rb, u_s) + _mm3(m_rk, v_s)
            y4 = y_s[0:cl_]
            for hh in range(1, n // cl_):
                y4 = y4 + y_s[hh * cl_:(hh + 1) * cl_]
            y_ref[rows, ls] = y4
            colg = jnp.sum(jnp.where(eye, jnp.broadcast_to(gam[:, ls], (n, n)), 0.0), axis=1, keepdims=True)
            st_ref[grp] = colg * st + _mm3(jnp.concatenate([bg_s, kg_s], axis=0),
                                           jnp.concatenate([u_s, v_s], axis=0), _TN)
        return 0

    lax.fori_loop(0, ts // cl_, chunk, 0)

    y = y_ref[...]
    mean = _mm_exact_rhs(y, havg_ref[...])
    yc = y - mean
    var = _mm_exact_rhs(yc * yc, havg_ref[...])
    y = yc * lax.rsqrt(var + RWKV_GN_EPS) * lnw_ref[...] + lnb_ref[...]
    o_ref[0] = (y + bonus_ref[0]) * g_ref[0]


def rwkv7_time_mix(proj, mu, w0, w_up, a0, a_up, g_up, k_k, k_a, r_k, ln_w, ln_b, *, ts):
    bsz, seq, _ = proj.shape
    w = RWKV_WIDTH
    pw = 3 * w + 2 * LANES
    assert RWKV_W_LORA + RWKV_A_LORA == LANES and RWKV_G_LORA == LANES and seq % ts == 0
    wup = jnp.concatenate([w_up, jnp.zeros((RWKV_A_LORA, w), F32)], axis=0)
    aup = jnp.concatenate([jnp.zeros((RWKV_W_LORA, w), F32), a_up], axis=0)
    hsum = _head_sum_matrix(w)
    row = lambda t: t.reshape(1, -1)
    full = lambda shape: pl.BlockSpec(shape, lambda b, s: (0,) * len(shape))
    tile = pl.BlockSpec((1, ts, w), lambda b, s: (b, s, 0))
    nprev = ts // SUBLANES
    outs = pl.pallas_call(
        _rwkv_prep_kernel,
        grid=(bsz, seq // ts),
        in_specs=[
            pl.BlockSpec((1, ts, pw), lambda b, s: (b, s, 0)),
            pl.BlockSpec((1, SUBLANES, pw), lambda b, s: (b, jnp.maximum(s * nprev - 1, 0), 0)),
            full((1, pw)), full((1, w)), full((LANES, w)), full((1, w)), full((LANES, w)),
            full((LANES, w)), full((1, w)), full((1, w)), full((1, w)), full((w, w)),
        ],
        out_specs=[tile] * 8,
        out_shape=[jax.ShapeDtypeStruct((bsz, seq, w), F32)] * 8,
        compiler_params=_cparams(("parallel", "parallel")),
        name="rwkv_prep",
    )(proj, proj, row(mu), row(w0), wup, row(a0), aup, g_up.astype(BF16), row(k_k), row(k_a),
      row(r_k), hsum)
    havg = (_head_sum_matrix(w).astype(F32) / HEAD_DIM).astype(BF16)
    return pl.pallas_call(
        _rwkv_scan_kernel,
        grid=(bsz, seq // ts),
        in_specs=[tile] * 8 + [full((1, w)), full((1, w)), full((w, w))],
        out_specs=tile,
        out_shape=jax.ShapeDtypeStruct((bsz, seq, w), F32),
        scratch_shapes=[pltpu.VMEM((ts, w), F32),
                        pltpu.VMEM((w // RWKV_GROUP, RWKV_GROUP, RWKV_GROUP), F32)],
        compiler_params=_cparams(("parallel", "arbitrary")),
        name="rwkv_scan",
    )(*outs, row(ln_w), row(ln_b), havg)


DSA_KV_TILE = 512
INT32_MIN = -2 ** 31
NEG_INF_KEY = -2139095041


def _sortable_key(x):
    bits = pltpu.bitcast(x + 0.0, jnp.int32)
    return jnp.where(bits < 0, bits ^ jnp.int32(0x7FFFFFFF), bits)


def _dsa_kernel(q_ref, qi_ref, wi_ref, k_ref, v_ref, ki_ref, o_ref,
                kb_ref, vb_ref, kih_ref, kil_ref, key_ref, bias_ref):
    i = pl.program_id(1)
    qb_, kt = DSA_QBLOCK, DSA_KV_TILE
    seq = k_ref.shape[1]
    topk = min(DSA_TOPK, seq // 4)

    @pl.when(i == 0)
    def _():
        kb_ref[...] = k_ref[0].astype(BF16)
        vb_ref[...] = v_ref[0].astype(BF16)
        hi, lo = _split(ki_ref[0])
        kih_ref[...] = hi
        kil_ref[...] = lo

    ntile = (i * qb_) // kt + 1
    lane = lax.broadcasted_iota(jnp.int32, (qb_, LANES), 1)
    qpos = i * qb_ + lax.broadcasted_iota(jnp.int32, (qb_, kt), 0)
    kcol = lax.broadcasted_iota(jnp.int32, (qb_, kt), 1)
    wi = wi_ref[0] * (DSA_IDX_HEADS ** -0.5 * HEAD_DIM ** -0.5)
    w_cols = [jnp.broadcast_to(wi[:, h:h + 1], (qb_, kt)) for h in range(DSA_IDX_HEADS)]

    qi_parts = []
    for h in range(DSA_IDX_HEADS):
        pair = qi_ref[0, :, (h // 2) * LANES:(h // 2 + 1) * LANES]
        qi_parts.append(_split(jnp.where(lane // HEAD_DIM == h % 2, pair, 0.0)))

    def score_tile(c, _):
        cols = pl.ds(pl.multiple_of(c * kt, kt), kt)
        kih, kil = kih_ref[cols, :], kil_ref[cols, :]
        score = jnp.zeros((qb_, kt), F32)
        for h in range(DSA_IDX_HEADS):
            qh, ql = qi_parts[h]
            logit = lax.dot_general(qh, kih, _NT, preferred_element_type=F32)
            logit += lax.dot_general(qh, kil, _NT, preferred_element_type=F32)
            logit += lax.dot_general(ql, kih, _NT, preferred_element_type=F32)
            score += jnp.maximum(logit, 0.0) * w_cols[h]
        score = jnp.where(c * kt + kcol <= qpos, score, -jnp.inf)
        key_ref[:, cols] = _sortable_key(score)
        return 0

    lax.fori_loop(0, ntile, score_tile, 0)

    def count(pred):
        def tile(c, acc):
            cols = pl.ds(pl.multiple_of(c * kt, kt), kt)
            hit = jnp.where(pred(key_ref[:, cols], c * kt + kcol), 1.0, 0.0)
            for part in range(kt // LANES):
                acc = acc + hit[:, part * LANES:(part + 1) * LANES]
            return acc
        acc = lax.fori_loop(0, ntile, tile, jnp.zeros((qb_, LANES), F32))
        return jnp.sum(acc, axis=1, keepdims=True)

    def search_step(t, base):
        cand = base + lax.shift_left(jnp.int32(1), 31 - t)
        cnt = count(lambda key, col: key >= cand)
        return jnp.where(cnt >= topk, cand, base)

    thr = lax.fori_loop(0, 32, search_step, jnp.full((qb_, 1), INT32_MIN, jnp.int32))
    need = topk - count(lambda key, col: key > thr)
    n_eq = count(lambda key, col: key == thr)
    tied = jnp.max(jnp.where((n_eq > need) & (thr > NEG_INF_KEY), 1.0, 0.0)) > 0.0
    nbits = seq.bit_length()

    def tie_search(_):
        def step(t, pos):
            cand = pos + lax.shift_left(jnp.int32(1), nbits - 1 - t)
            cnt = count(lambda key, col: (key == thr) & (col < cand))
            return jnp.where(cnt < need, cand, pos)
        return lax.fori_loop(0, nbits, step, jnp.zeros((qb_, 1), jnp.int32))

    last = lax.cond(tied, tie_search, lambda _: jnp.full((qb_, 1), seq, jnp.int32), 0)

    def bias_tile(c, _):
        cols = pl.ds(pl.multiple_of(c * kt, kt), kt)
        key = key_ref[:, cols]
        col = c * kt + kcol
        chosen = ((key > thr) | ((key == thr) & (col <= last))) & (col <= qpos)
        bias_ref[:, cols] = jnp.where(chosen, 0.0, NEG_BIG)
        return 0

    lax.fori_loop(0, ntile, bias_tile, 0)

    for pr in range(DSA_WIDTH // LANES):
        pair = q_ref[0, :, pr * LANES:(pr + 1) * LANES] * (HEAD_DIM ** -0.5)
        outs = []
        for hh in range(LANES // HEAD_DIM):
            qh = jnp.where(lane // HEAD_DIM == hh, pair, 0.0).astype(BF16)

            def attn_tile(c, carry, qh=qh):
                m_prev, l_prev, acc = carry
                cols = pl.ds(pl.multiple_of(c * kt, kt), kt)
                s = lax.dot_general(qh, kb_ref[cols, :], _NT, preferred_element_type=F32) + bias_ref[:, cols]
                m_new = jnp.maximum(m_prev, jnp.max(s, axis=1, keepdims=True))
                alpha = jnp.exp(m_prev - m_new)
                p = jnp.exp(s - m_new)
                l_new = alpha * l_prev + jnp.sum(p, axis=1, keepdims=True)
                acc = alpha * acc + jnp.dot(p.astype(BF16), vb_ref[cols, :], preferred_element_type=F32)
                return m_new, l_new, acc

            init = (jnp.full((qb_, 1), NEG_BIG, F32), jnp.zeros((qb_, 1), F32), jnp.zeros((qb_, LANES), F32))
            _, l_fin, acc = lax.fori_loop(0, ntile, attn_tile, init)
            outs.append(acc / l_fin)
        o_ref[0, :, pr * LANES:(pr + 1) * LANES] = jnp.where(lane < HEAD_DIM, outs[0], outs[1])


def dsa_attention(proj, *, q_col, qi_col, wi_col, k_col, v_col, ki_col):
    bsz, seq, _ = proj.shape
    assert seq % DSA_KV_TILE == 0 and DSA_KV_TILE % DSA_QBLOCK == 0
    nq = seq // DSA_QBLOCK
    kvspec = lambda col: pl.BlockSpec((1, seq, LANES), lambda b, i: (b, 0, col // LANES))
    return pl.pallas_call(
        _dsa_kernel,
        grid=(bsz, nq),
        in_specs=[
            pl.BlockSpec((1, DSA_QBLOCK, DSA_WIDTH), lambda b, i: (b, i, q_col // DSA_WIDTH)),
            pl.BlockSpec((1, DSA_QBLOCK, 2 * LANES), lambda b, i: (b, i, qi_col // (2 * LANES))),
            pl.BlockSpec((1, DSA_QBLOCK, LANES), lambda b, i: (b, i, wi_col // LANES)),
            kvspec(k_col), kvspec(v_col), kvspec(ki_col),
        ],
        out_specs=pl.BlockSpec((1, DSA_QBLOCK, DSA_WIDTH), lambda b, i: (b, i, 0)),
        out_shape=jax.ShapeDtypeStruct((bsz, seq, DSA_WIDTH), F32),
        scratch_shapes=[pltpu.VMEM((seq, LANES), BF16), pltpu.VMEM((seq, LANES), BF16),
                        pltpu.VMEM((seq, LANES), BF16), pltpu.VMEM((seq, LANES), BF16),
                        pltpu.VMEM((DSA_QBLOCK, seq), jnp.int32), pltpu.VMEM((DSA_QBLOCK, seq), F32)],
        compiler_params=_cparams(("parallel", "arbitrary")),
        name="dsa_attention",
    )(proj, proj, proj, proj, proj, proj)


MOE_TOKEN_TILE = 1024
MOE_ROW_BLOCK = 128


def _router_kernel(h_ref, g_ref, wr_ref, xn_ref, route_ref):
    xn = _rms(h_ref[...], g_ref[...])
    xn_ref[...] = xn.astype(BF16)
    lane = lax.broadcasted_iota(jnp.int32, (xn.shape[0], LANES), 1)
    logits = jnp.where(lane < N_EXPERTS, _mm3(xn, wr_ref[...]), -jnp.inf)
    v1 = jnp.max(logits, axis=1, keepdims=True)
    e1 = jnp.min(jnp.where(logits == v1, lane, LANES), axis=1, keepdims=True)
    rest = jnp.where(lane == e1, -jnp.inf, logits)
    v2 = jnp.max(rest, axis=1, keepdims=True)
    e2 = jnp.min(jnp.where(rest == v2, lane, LANES), axis=1, keepdims=True)
    ratio = jnp.exp(v2 - v1)
    g1 = 1.0 / (1.0 + ratio)
    g2 = ratio * g1
    route_ref[...] = jnp.where(lane == 0, e1.astype(F32), jnp.where(
        lane == 1, e2.astype(F32), jnp.where(lane == 2, g1, jnp.where(lane == 3, g2, 0.0))))


def route_tokens(h2d, g, router, *, tm):
    t, d = h2d.shape
    wr = jnp.zeros((d, LANES), F32).at[:, :N_EXPERTS].set(router)
    return pl.pallas_call(
        _router_kernel,
        grid=(t // tm,),
        in_specs=[pl.BlockSpec((tm, d), lambda i: (i, 0)), pl.BlockSpec((1, d), lambda i: (0, 0)),
                  pl.BlockSpec((d, LANES), lambda i: (0, 0))],
        out_specs=[pl.BlockSpec((tm, d), lambda i: (i, 0)), pl.BlockSpec((tm, LANES), lambda i: (i, 0))],
        out_shape=[jax.ShapeDtypeStruct((t, d), BF16), jax.ShapeDtypeStruct((t, LANES), F32)],
        compiler_params=_cparams(("parallel",)),
        name="moe_router",
    )(h2d, g.reshape(1, d), wr)


def _moe_kernel(cnt_ref, xn_ref, rk_ref, gt_ref, rkt_ref, wg_ref, wu_ref, wd_ref, o_ref, xs_ref, acc_ref):
    tau, e, f = pl.program_id(0), pl.program_id(1), pl.program_id(2)
    tt = xn_ref.shape[0]
    rb = MOE_ROW_BLOCK
    nblk = (cnt_ref[tau * N_EXPERTS + e] + rb - 1) // rb
    slot = lax.broadcasted_iota(jnp.int32, (rb, tt), 0)

    @pl.when((e == 0) & (f == 0))
    def _():
        o_ref[...] = jnp.zeros_like(o_ref)

    def rows_of(b):
        return pl.ds(pl.multiple_of(b * rb, rb), rb)

    @pl.when(f == 0)
    def _():
        rank = rk_ref[0, pl.ds(e, 1), :]

        def gather(b, _):
            onehot = jnp.where(rank == slot + b * rb, 1.0, 0.0).astype(BF16)
            xs_ref[rows_of(b), :] = jnp.dot(onehot, xn_ref[...], preferred_element_type=F32).astype(BF16)
            acc_ref[rows_of(b), :] = jnp.zeros((rb, acc_ref.shape[1]), F32)
            return 0

        lax.fori_loop(0, nblk, gather, 0)

    def ffn(b, _):
        xs = xs_ref[rows_of(b), :]
        gate = jnp.dot(xs, wg_ref[0], preferred_element_type=F32)
        up = jnp.dot(xs, wu_ref[0], preferred_element_type=F32)
        act = (gate * jax.nn.sigmoid(gate) * up).astype(BF16)
        acc_ref[rows_of(b), :] += jnp.dot(act, wd_ref[0], preferred_element_type=F32)
        return 0

    lax.fori_loop(0, nblk, ffn, 0)

    @pl.when(f == pl.num_programs(2) - 1)
    def _():
        rank = rk_ref[0, pl.ds(e, 1), :]
        gates = gt_ref[0, pl.ds(e, 1), :]
        lane_e = lax.broadcasted_iota(jnp.int32, rkt_ref.shape[1:], 1)
        rank_t = jnp.sum(jnp.where(lane_e == e, rkt_ref[0], 0), axis=1, keepdims=True)
        slot_t = lax.broadcasted_iota(jnp.int32, (tt, rb), 1)

        def combine(b, _):
            hit = rank == slot + b * rb
            gate_slot = jnp.sum(jnp.where(hit, gates, 0.0), axis=1, keepdims=True)
            yg = (acc_ref[rows_of(b), :] * gate_slot).astype(BF16)
            scatter = jnp.where(rank_t == slot_t + b * rb, 1.0, 0.0).astype(BF16)
            o_ref[...] += jnp.dot(scatter, yg, preferred_element_type=F32)
            return 0

        lax.fori_loop(0, nblk, combine, 0)


def moe_experts(xn_bf16, route, w_gate, w_up, w_down, *, tf):
    t, d = xn_bf16.shape
    tt = MOE_TOKEN_TILE
    ntile = t // tt
    fdim = w_gate.shape[2]
    experts = route[:, 0:2].astype(jnp.int32).reshape(ntile, tt * 2)
    gates = route[:, 2:4].reshape(ntile, tt * 2)
    onehot = (experts[:, :, None] == jnp.arange(N_EXPERTS)[None, None, :]).astype(jnp.int32)
    csum = jnp.cumsum(onehot, axis=1)
    counts = csum[:, -1, :]
    rank = jnp.where(onehot > 0, csum - 1, -1)
    rank_tok = jnp.max(rank.reshape(ntile, tt, 2, N_EXPERTS), axis=2)
    gate_tok = jnp.sum((onehot * gates[:, :, None]).reshape(ntile, tt, 2, N_EXPERTS), axis=2)
    rk = jnp.swapaxes(rank_tok, 1, 2)
    gt = jnp.swapaxes(gate_tok, 1, 2)
    grid_spec = pltpu.PrefetchScalarGridSpec(
        num_scalar_prefetch=1,
        grid=(ntile, N_EXPERTS, fdim // tf),
        in_specs=[
            pl.BlockSpec((tt, d), lambda i, e, f, c: (i, 0)),
            pl.BlockSpec((1, N_EXPERTS, tt), lambda i, e, f, c: (i, 0, 0)),
            pl.BlockSpec((1, N_EXPERTS, tt), lambda i, e, f, c: (i, 0, 0)),
            pl.BlockSpec((1, tt, N_EXPERTS), lambda i, e, f, c: (i, 0, 0)),
            pl.BlockSpec((1, d, tf), lambda i, e, f, c: (e, 0, f)),
            pl.BlockSpec((1, d, tf), lambda i, e, f, c: (e, 0, f)),
            pl.BlockSpec((1, tf, d), lambda i, e, f, c: (e, f, 0)),
        ],
        out_specs=pl.BlockSpec((tt, d), lambda i, e, f, c: (i, 0)),
        scratch_shapes=[pltpu.VMEM((tt, d), BF16), pltpu.VMEM((tt, d), F32)],
    )
    return pl.pallas_call(
        _moe_kernel,
        grid_spec=grid_spec,
        out_shape=jax.ShapeDtypeStruct((t, d), F32),
        compiler_params=_cparams(("parallel", "arbitrary", "arbitrary")),
        name="moe_experts",
    )(counts.reshape(-1), xn_bf16, rk, gt, rank_tok, w_gate, w_up, w_down)


def _add_norm_kernel(a_ref, b_ref, g_ref, o_ref, *, normalize):
    y = a_ref[...] + b_ref[...]
    o_ref[...] = _rms(y, g_ref[...]) if normalize else y


def add_rmsnorm(a, b, g, *, tm):
    t, d = a.shape
    spec = pl.BlockSpec((tm, d), lambda i: (i, 0))
    gain = jnp.ones((1, d), F32) if g is None else g.reshape(1, d)
    return pl.pallas_call(
        functools.partial(_add_norm_kernel, normalize=g is not None),
        grid=(t // tm,),
        in_specs=[spec, spec, pl.BlockSpec((1, d), lambda i: (0, 0))],
        out_specs=spec,
        out_shape=jax.ShapeDtypeStruct((t, d), F32),
        compiler_params=_cparams(("parallel",)),
        name="add_rmsnorm",
    )(a, b, gain)


ROW_TILE = 512

EVEN_Q_COL, EVEN_K_COL, EVEN_V_COL = 512, 1024, 1536
EVEN_ROPE_COLS = np.zeros(2048, bool)
EVEN_ROPE_COLS[EVEN_Q_COL:EVEN_V_COL] = True


def _odd_layout():
    rw = 3 * RWKV_WIDTH + RWKV_W_LORA + RWKV_A_LORA + RWKV_G_LORA
    q0 = rw
    k0 = q0 + DSA_WIDTH
    v0 = k0 + HEAD_DIM
    qi0 = v0 + HEAD_DIM
    ki0 = qi0 + DSA_IDX_HEADS * HEAD_DIM
    wi0 = ki0 + HEAD_DIM
    seg = lambda start, n: list(range(start, start + n))
    src, keep, rope = [], [], []

    def add(cols, roped, pad=0):
        src.extend(cols + [0] * pad)
        keep.extend([1.0] * len(cols) + [0.0] * pad)
        rope.extend([roped] * (len(cols) + pad))

    add(seg(0, rw), False)
    cols = {"k_col": len(src)}
    add(seg(k0, HEAD_DIM) * 2, True)
    cols["v_col"] = len(src)
    add(seg(v0, HEAD_DIM) * 2, False)
    cols["q_col"] = len(src)
    add(seg(q0, DSA_WIDTH), True)
    cols["qi_col"] = len(src)
    add(seg(qi0, DSA_IDX_HEADS * HEAD_DIM), True)
    cols["ki_col"] = len(src)
    add(seg(ki0, HEAD_DIM) * 2, True)
    cols["wi_col"] = len(src)
    add(seg(wi0, DSA_IDX_HEADS), False, pad=LANES - DSA_IDX_HEADS)
    return (np.asarray(src, np.int32), np.asarray(keep, np.float32), np.asarray(rope, bool), cols)


ODD_COL_SRC, ODD_COL_KEEP, ODD_ROPE_COLS, ODD_DSA_COLS = _odd_layout()


def kernel(x, e_norm_mix, e_w_in, s5_a_re, s5_a_im, s5_log_dt, s5_b_re, s5_b_im, s5_c_re, s5_c_im, s5_d, s5_glu_w, s5_glu_b, e_w_out, e_norm_ffn, ffn_w_gate, ffn_w_up, ffn_w_down, o_norm_mix, o_w_in, rwkv_mu, rwkv_w0, rwkv_w_up, rwkv_a0, rwkv_a_up, rwkv_g_up, rwkv_k_k, rwkv_k_a, rwkv_r_k, rwkv_ln_w, rwkv_ln_b, o_w_out, o_norm_ffn, moe_router, moe_w_gate, moe_w_up, moe_w_down, final_norm):
    bsz, seq, d = x.shape
    n_even, n_odd = e_norm_mix.shape[0], o_norm_mix.shape[0]
    h = x.reshape(-1, d)
    zero = jnp.zeros((bsz * seq, d), F32)
    for layer in range(n_even + n_odd):
        j = layer // 2
        last = layer == n_even + n_odd - 1
        if layer % 2 == 0:
            proj = norm_matmul_rope(h, e_norm_mix[j], e_w_in[j].astype(BF16), EVEN_ROPE_COLS, seq,
                                    tm=ROW_TILE, tn=512).reshape(bsz, seq, -1)
            ya = s5_mixer(proj, s5_a_re[j], s5_a_im[j], s5_log_dt[j], s5_b_re[j], s5_b_im[j], s5_c_re[j],
                          s5_c_im[j], s5_d[j], s5_glu_w[j], s5_glu_b[j], ts=256)
            yb = moba_attention(proj, q_col=EVEN_Q_COL, k_col=EVEN_K_COL, v_col=EVEN_V_COL)
            h = out_proj_residual(h, ya.reshape(-1, S5_WIDTH), yb.reshape(-1, MOBA_WIDTH),
                                  e_w_out[j].astype(BF16), tm=ROW_TILE)
            h = dense_ffn_residual(h, e_norm_ffn[j], ffn_w_gate[j].astype(BF16), ffn_w_up[j].astype(BF16),
                                   ffn_w_down[j].astype(BF16), tm=ROW_TILE, tf=256)
            if last:
                h = add_rmsnorm(h, zero, final_norm, tm=ROW_TILE)
        else:
            w_in = jnp.take(o_w_in[j], jnp.asarray(ODD_COL_SRC), axis=1) * jnp.asarray(ODD_COL_KEEP)[None, :]
            proj = norm_matmul_rope(h, o_norm_mix[j], w_in.astype(BF16), ODD_ROPE_COLS, seq,
                                    tm=ROW_TILE, tn=512).reshape(bsz, seq, -1)
            yc = rwkv7_time_mix(proj, rwkv_mu[j], rwkv_w0[j], rwkv_w_up[j], rwkv_a0[j], rwkv_a_up[j],
                                rwkv_g_up[j], rwkv_k_k[j], rwkv_k_a[j], rwkv_r_k[j], rwkv_ln_w[j],
                                rwkv_ln_b[j], ts=256)
            yd = dsa_attention(proj, **ODD_DSA_COLS)
            h = out_proj_residual(h, yc.reshape(-1, RWKV_WIDTH), yd.reshape(-1, DSA_WIDTH),
                                  o_w_out[j].astype(BF16), tm=ROW_TILE)
            xn, route = route_tokens(h, o_norm_ffn[j], moe_router[j], tm=ROW_TILE)
            y = moe_experts(xn, route, moe_w_gate[j].astype(BF16), moe_w_up[j].astype(BF16),
                            moe_w_down[j].astype(BF16), tf=512)
            h = add_rmsnorm(h, y, final_norm if last else None, tm=ROW_TILE)
    return h.reshape(bsz, seq, d)
```

```python
import functools
import math

import jax
import jax.numpy as jnp
import numpy as np
from jax import lax
from jax.experimental import pallas as pl
from jax.experimental.pallas import tpu as pltpu

F32 = jnp.float32
BF16 = jnp.bfloat16
HIGHEST = lax.Precision.HIGHEST

LANES = 128
SUBLANES = 8
VMEM_LIMIT_BYTES = 56 * 1024 * 1024

D_MODEL = 1024
HEAD_DIM = 64
ROT_DIM = HEAD_DIM // 4
ROPE_THETA = 500000.0
NORM_EPS = 1e-6
S5_WIDTH = 512
S5_GROUP = 16
S5_GROUPS = S5_WIDTH // S5_GROUP
S5_STATE = 64
S5_NSTATE = S5_GROUPS * S5_STATE
MOBA_WIDTH = 512
MOBA_HEADS = MOBA_WIDTH // HEAD_DIM
MOBA_BLOCK = 256
MOBA_TOPK = 3
MOBA_BLOCKS_PER_STEP = 4
RWKV_WIDTH = 512
RWKV_HEADS = RWKV_WIDTH // HEAD_DIM
RWKV_W_LORA = 64
RWKV_A_LORA = 64
RWKV_G_LORA = 128
RWKV_GN_EPS = 64e-5
RWKV_CHUNK = 64
DSA_WIDTH = 512
DSA_HEADS = DSA_WIDTH // HEAD_DIM
DSA_IDX_HEADS = 4
DSA_TOPK = 256
DSA_QBLOCK = 128
N_EXPERTS = 8
NEG_BIG = -1e30
LOG2_E = 1.4426950408889634


def _cparams(sem):
    return pltpu.CompilerParams(dimension_semantics=sem, vmem_limit_bytes=VMEM_LIMIT_BYTES)


def _rms(x, g):
    return x * lax.rsqrt(jnp.mean(x * x, axis=-1, keepdims=True) + NORM_EPS) * g


def _norm_matmul_kernel(flags_ref, h_ref, g_ref, w_ref, cos_ref, sina_ref, sinb_ref, colf_ref,
                        o_ref, xn_ref):
    j = pl.program_id(1)

    @pl.when(j == 0)
    def _():
        xn_ref[...] = _rms(h_ref[...], g_ref[...]).astype(BF16)

    acc = jnp.dot(xn_ref[...], w_ref[...], preferred_element_type=F32)
    tn = acc.shape[1]

    @pl.when(flags_ref[j] == 0)
    def _():
        o_ref[...] = acc

    @pl.when(flags_ref[j] != 0)
    def _():
        reps = tn // LANES
        colf = colf_ref[...]
        cos = 1.0 + colf * (jnp.tile(cos_ref[...], (1, reps)) - 1.0)
        sina = colf * jnp.tile(sina_ref[...], (1, reps))
        sinb = colf * jnp.tile(sinb_ref[...], (1, reps))
        nxt = pltpu.roll(acc, tn - ROT_DIM // 2, axis=1)
        prv = pltpu.roll(acc, ROT_DIM // 2, axis=1)
        o_ref[...] = acc * cos + nxt * sina + prv * sinb


def _rope_tables(seq):
    half = ROT_DIM // 2
    inv_freq = ROPE_THETA ** (-jnp.arange(half, dtype=F32) / half)
    ang = jnp.arange(seq, dtype=F32)[:, None] * inv_freq[None, :]
    cos, sin = jnp.cos(ang), jnp.sin(ang)
    ones = jnp.ones((seq, HEAD_DIM - ROT_DIM), F32)
    zeros = jnp.zeros((seq, HEAD_DIM - ROT_DIM), F32)
    zh = jnp.zeros((seq, half), F32)
    cos64 = jnp.concatenate([cos, cos, ones], axis=1)
    sina64 = jnp.concatenate([-sin, zh, zeros], axis=1)
    sinb64 = jnp.concatenate([zh, sin, zeros], axis=1)
    rep = LANES // HEAD_DIM
    return jnp.tile(cos64, (1, rep)), jnp.tile(sina64, (1, rep)), jnp.tile(sinb64, (1, rep))


def norm_matmul_rope(h2d, g, w_bf16, rope_cols, seq, *, tm, tn):
    t, d = h2d.shape
    n = w_bf16.shape[1]
    assert t % tm == 0 and n % tn == 0 and seq % tm == 0 and tn % LANES == 0
    nseq = seq // tm
    cos, sina, sinb = _rope_tables(seq)
    tile_flags = jnp.asarray(rope_cols.reshape(n // tn, tn).any(axis=1).astype(np.int32))
    colf = jnp.asarray(rope_cols.astype(np.float32))[None, :]
    grid_spec = pltpu.PrefetchScalarGridSpec(
        num_scalar_prefetch=1,
        grid=(t // tm, n // tn),
        in_specs=[
            pl.BlockSpec((tm, d), lambda i, j, f: (i, 0)),
            pl.BlockSpec((1, d), lambda i, j, f: (0, 0)),
            pl.BlockSpec((d, tn), lambda i, j, f: (0, j)),
            pl.BlockSpec((tm, LANES), lambda i, j, f: (i % nseq, 0)),
            pl.BlockSpec((tm, LANES), lambda i, j, f: (i % nseq, 0)),
            pl.BlockSpec((tm, LANES), lambda i, j, f: (i % nseq, 0)),
            pl.BlockSpec((1, tn), lambda i, j, f: (0, j)),
        ],
        out_specs=pl.BlockSpec((tm, tn), lambda i, j, f: (i, j)),
        scratch_shapes=[pltpu.VMEM((tm, d), BF16)],
    )
    return pl.pallas_call(
        _norm_matmul_kernel,
        grid_spec=grid_spec,
        out_shape=jax.ShapeDtypeStruct((t, n), F32),
        compiler_params=_cparams(("parallel", "arbitrary")),
        name="norm_matmul_rope",
    )(tile_flags, h2d, g.reshape(1, d), w_bf16, cos, sina, sinb, colf)


def _out_proj_kernel(res_ref, a_ref, b_ref, wa_ref, wb_ref, o_ref):
    acc = jnp.dot(a_ref[...].astype(BF16), wa_ref[...], preferred_element_type=F32)
    acc += jnp.dot(b_ref[...].astype(BF16), wb_ref[...], preferred_element_type=F32)
    o_ref[...] = res_ref[...] + acc


def out_proj_residual(res, a, b, w_bf16, *, tm):
    t, d = res.shape
    ka, kb = a.shape[1], b.shape[1]
    wa, wb = w_bf16[:ka], w_bf16[ka:]
    return pl.pallas_call(
        _out_proj_kernel,
        grid=(t // tm,),
        in_specs=[
            pl.BlockSpec((tm, d), lambda i: (i, 0)),
            pl.BlockSpec((tm, ka), lambda i: (i, 0)),
            pl.BlockSpec((tm, kb), lambda i: (i, 0)),
            pl.BlockSpec((ka, d), lambda i: (0, 0)),
            pl.BlockSpec((kb, d), lambda i: (0, 0)),
        ],
        out_specs=pl.BlockSpec((tm, d), lambda i: (i, 0)),
        out_shape=jax.ShapeDtypeStruct((t, d), F32),
        compiler_params=_cparams(("parallel",)),
        name="out_proj_residual",
    )(res, a, b, wa, wb)


def _dense_ffn_kernel(h_ref, g_ref, wg_ref, wu_ref, wd_ref, o_ref, xn_ref, acc_ref):
    f = pl.program_id(1)

    @pl.when(f == 0)
    def _():
        xn_ref[...] = _rms(h_ref[...], g_ref[...]).astype(BF16)
        acc_ref[...] = jnp.zeros_like(acc_ref)

    xn = xn_ref[...]
    gate = jnp.dot(xn, wg_ref[...], preferred_element_type=F32)
    up = jnp.dot(xn, wu_ref[...], preferred_element_type=F32)
    act = (gate * jax.nn.sigmoid(gate) * up).astype(BF16)
    acc_ref[...] += jnp.dot(act, wd_ref[...], preferred_element_type=F32)

    @pl.when(f == pl.num_programs(1) - 1)
    def _():
        o_ref[...] = h_ref[...] + acc_ref[...]


def dense_ffn_residual(h2d, g, wg, wu, wd, *, tm, tf):
    t, d = h2d.shape
    fdim = wg.shape[1]
    assert t % tm == 0 and fdim % tf == 0
    return pl.pallas_call(
        _dense_ffn_kernel,
        grid=(t // tm, fdim // tf),
        in_specs=[
            pl.BlockSpec((tm, d), lambda i, f: (i, 0)),
            pl.BlockSpec((1, d), lambda i, f: (0, 0)),
            pl.BlockSpec((d, tf), lambda i, f: (0, f)),
            pl.BlockSpec((d, tf), lambda i, f: (0, f)),
            pl.BlockSpec((tf, d), lambda i, f: (f, 0)),
        ],
        out_specs=pl.BlockSpec((tm, d), lambda i, f: (i, 0)),
        out_shape=jax.ShapeDtypeStruct((t, d), F32),
        scratch_shapes=[pltpu.VMEM((tm, d), BF16), pltpu.VMEM((tm, d), F32)],
        compiler_params=_cparams(("parallel", "arbitrary")),
        name="dense_ffn_residual",
    )(h2d, g.reshape(1, d), wg, wu, wd)


S5_LANE_CHUNK = 512


def _s5_kernel(u_ref, bre_ref, bim_ref, cre_ref, cim_ref, stepr_ref, stepi_ref, carr_ref, cari_ref,
               d_ref, gw_ref, gb_ref, o_ref, xr_ref, xi_ref, sr_ref, si_ref):
    s = pl.program_id(1)
    ts = u_ref.shape[1]

    @pl.when(s == 0)
    def _():
        sr_ref[...] = jnp.zeros_like(sr_ref)
        si_ref[...] = jnp.zeros_like(si_ref)

    u = u_ref[0]
    ub = u.astype(BF16)
    xr_ref[...] = jnp.dot(ub, bre_ref[...], preferred_element_type=F32)
    xi_ref[...] = jnp.dot(ub, bim_ref[...], preferred_element_type=F32)

    for c in range(S5_NSTATE // S5_LANE_CHUNK):
        lanes = pl.ds(c * S5_LANE_CHUNK, S5_LANE_CHUNK)
        steps = [(stepr_ref[k, :, lanes], stepi_ref[k, :, lanes]) for k in range(3)]
        pr, pi = carr_ref[:, lanes], cari_ref[:, lanes]

        def body(g, carry, lanes=lanes, steps=steps, pr=pr, pi=pi):
            cr, ci = carry
            rows = pl.ds(pl.multiple_of(g * SUBLANES, SUBLANES), SUBLANES)
            xr = xr_ref[rows, lanes]
            xi = xi_ref[rows, lanes]
            for k, (ar, ai) in enumerate(steps):
                rr = pltpu.roll(xr, 1 << k, axis=0)
                ri = pltpu.roll(xi, 1 << k, axis=0)
                xr, xi = xr + ar * rr - ai * ri, xi + ar * ri + ai * rr
            xr = xr + pr * cr - pi * ci
            xi = xi + pr * ci + pi * cr
            xr_ref[rows, lanes] = xr
            xi_ref[rows, lanes] = xi
            return xr[SUBLANES - 1:SUBLANES, :], xi[SUBLANES - 1:SUBLANES, :]

        cr, ci = lax.fori_loop(0, ts // SUBLANES, body, (sr_ref[:, lanes], si_ref[:, lanes]))
        sr_ref[:, lanes] = cr
        si_ref[:, lanes] = ci

    y = jnp.dot(xr_ref[...].astype(BF16), cre_ref[...], preferred_element_type=F32)
    y -= jnp.dot(xi_ref[...].astype(BF16), cim_ref[...], preferred_element_type=F32)
    y = jax.nn.gelu(y + d_ref[...] * u)
    z = jnp.dot(y.astype(BF16), gw_ref[...], preferred_element_type=F32) + gb_ref[...]
    o_ref[0] = y * jax.nn.sigmoid(z)


def _s5_params(a_re, a_im, log_dt, b_re, b_im, c_re, c_im):
    dt = jnp.exp(log_dt)[:, None]

    def apow(n):
        mag = jnp.exp(n * dt * a_re)
        return mag * jnp.cos(n * dt * a_im), mag * jnp.sin(n * dt * a_im)

    abar_re, abar_im = apow(1.0)
    den = a_re * a_re + a_im * a_im
    xr, xi = abar_re - 1.0, abar_im
    f_re = (xr * a_re + xi * a_im) / den
    f_im = (xi * a_re - xr * a_im) / den
    bbar_re = f_re[..., None] * b_re - f_im[..., None] * b_im
    bbar_im = f_re[..., None] * b_im + f_im[..., None] * b_re
    eye = jnp.eye(S5_GROUPS, dtype=F32)

    def blockdiag_in(b):
        return jnp.einsum('gpc,gh->gchp', b, eye).reshape(S5_WIDTH, S5_NSTATE)

    def blockdiag_out(c):
        return jnp.einsum('gcp,gh->gphc', c, eye).reshape(S5_NSTATE, S5_WIDTH)

    row = jnp.arange(SUBLANES)[:, None]
    step_r, step_i = [], []
    for k in range(3):
        pr, pi = apow(float(1 << k))
        keep = row >= (1 << k)
        step_r.append(jnp.where(keep, pr.reshape(1, -1), 0.0))
        step_i.append(jnp.where(keep, pi.reshape(1, -1), 0.0))
    car = [apow(float(r + 1)) for r in range(SUBLANES)]
    car_r = jnp.stack([p[0].reshape(-1) for p in car])
    car_i = jnp.stack([p[1].reshape(-1) for p in car])
    return (blockdiag_in(bbar_re).astype(BF16), blockdiag_in(bbar_im).astype(BF16),
            blockdiag_out(c_re).astype(BF16), blockdiag_out(c_im).astype(BF16),
            jnp.stack(step_r), jnp.stack(step_i), car_r, car_i)


def s5_mixer(proj, a_re, a_im, log_dt, b_re, b_im, c_re, c_im, d, glu_w, glu_b, *, ts):
    bsz, seq, _ = proj.shape
    bre, bim, cre, cim, step_r, step_i, car_r, car_i = _s5_params(a_re, a_im, log_dt, b_re, b_im, c_re, c_im)
    full = lambda shape: pl.BlockSpec(shape, lambda b, s: (0,) * len(shape))
    return pl.pallas_call(
        _s5_kernel,
        grid=(bsz, seq // ts),
        in_specs=[
            pl.BlockSpec((1, ts, S5_WIDTH), lambda b, s: (b, s, 0)),
            full((S5_WIDTH, S5_NSTATE)), full((S5_WIDTH, S5_NSTATE)),
            full((S5_NSTATE, S5_WIDTH)), full((S5_NSTATE, S5_WIDTH)),
            full((3, SUBLANES, S5_NSTATE)), full((3, SUBLANES, S5_NSTATE)),
            full((SUBLANES, S5_NSTATE)), full((SUBLANES, S5_NSTATE)),
            full((1, S5_WIDTH)), full((S5_WIDTH, S5_WIDTH)), full((1, S5_WIDTH)),
        ],
        out_specs=pl.BlockSpec((1, ts, S5_WIDTH), lambda b, s: (b, s, 0)),
        out_shape=jax.ShapeDtypeStruct((bsz, seq, S5_WIDTH), F32),
        scratch_shapes=[pltpu.VMEM((ts, S5_NSTATE), F32), pltpu.VMEM((ts, S5_NSTATE), F32),
                        pltpu.VMEM((1, S5_NSTATE), F32), pltpu.VMEM((1, S5_NSTATE), F32)],
        compiler_params=_cparams(("parallel", "arbitrary")),
        name="s5_mixer",
    )(proj, bre, bim, cre, cim, step_r, step_i, car_r, car_i,
      d.reshape(1, -1), glu_w.astype(BF16), glu_b.reshape(1, -1))


_NT = (((1,), (1,)), ((), ()))
_NN = (((1,), (0,)), ((), ()))
_TN = (((0,), (0,)), ((), ()))


def _split(x):
    hi = x.astype(BF16)
    return hi, (x - hi.astype(F32)).astype(BF16)


def _mm(a, b, dims=_NN):
    return lax.dot_general(a.astype(BF16), b.astype(BF16), dims, preferred_element_type=F32)


def _mm3(a, b, dims=_NN):
    ah, al = _split(a)
    bh, bl = _split(b)
    out = lax.dot_general(ah, bh, dims, preferred_element_type=F32)
    out += lax.dot_general(ah, bl, dims, preferred_element_type=F32)
    out += lax.dot_general(al, bh, dims, preferred_element_type=F32)
    return out


def _mm_exact_rhs(a, b01, dims=_NN):
    ah, al = _split(a)
    out = lax.dot_general(ah, b01, dims, preferred_element_type=F32)
    out += lax.dot_general(al, b01, dims, preferred_element_type=F32)
    return out


def _moba_kernel(q_ref, k_ref, v_ref, o_ref, kmean_ref, kb_ref, vt_ref, bias_ref):
    i = pl.program_id(2)
    blk = MOBA_BLOCK
    seq = k_ref.shape[1]
    nb = seq // blk
    nh = LANES // HEAD_DIM

    @pl.when(i == 0)
    def _():
        k = k_ref[0]
        kmean_ref[...] = jnp.mean(k.reshape(nb, blk, LANES), axis=1)
        kb_ref[...] = k.astype(BF16)
        vt_ref[...] = v_ref[0].T.astype(BF16)

    q = q_ref[0]
    lane = lax.broadcasted_iota(jnp.int32, (blk, LANES), 1)
    brow = lax.broadcasted_iota(jnp.int32, (nb, blk), 0)
    krow = lax.broadcasted_iota(jnp.int32, (blk, blk), 0)
    qcol = lax.broadcasted_iota(jnp.int32, (blk, blk), 1)
    own = pl.ds(pl.multiple_of(i * blk, blk), blk)
    k_own, vt_own = kb_ref[own, :], vt_ref[:, own]
    qbs, state = [], []
    for hh in range(nh):
        qh = jnp.where(lane // HEAD_DIM == hh, q, 0.0)
        work = jnp.where(brow < i, _mm3(kmean_ref[...], qh, _NT), -jnp.inf)
        sel = jnp.zeros((nb, blk), F32)
        for _ in range(MOBA_TOPK):
            m = jnp.max(work, axis=0, keepdims=True)
            first = jnp.min(jnp.where(work == m, brow, nb), axis=0, keepdims=True)
            pick = (brow == first) & (m > -jnp.inf)
            sel = jnp.where(pick, 1.0, sel)
            work = jnp.where(pick, -jnp.inf, work)
        bias_ref[hh] = jnp.where(sel > 0.0, 0.0, NEG_BIG)

        qb = (qh * (HEAD_DIM ** -0.5 * LOG2_E)).astype(BF16)
        s = lax.dot_general(k_own, qb, _NT, preferred_element_type=F32)
        s = jnp.where(krow <= qcol, s, NEG_BIG)
        m0 = jnp.max(s, axis=0, keepdims=True)
        p = jnp.exp2(s - m0)
        acc0 = jnp.dot(vt_own, p.astype(BF16), preferred_element_type=F32)
        qbs.append(qb)
        state += [m0, jnp.sum(p, axis=0, keepdims=True), acc0]

    grp = MOBA_BLOCKS_PER_STEP

    def body(t, carry):
        rows = pl.ds(pl.multiple_of(t * (grp * blk), grp * blk), grp * blk)
        kj, vtj = kb_ref[rows, :], vt_ref[:, rows]
        out = []
        for hh in range(nh):
            m_prev, l_prev, acc = carry[3 * hh:3 * hh + 3]
            s = lax.dot_general(kj, qbs[hh], _NT, preferred_element_type=F32)
            s = jnp.concatenate([s[g * blk:(g + 1) * blk] + bias_ref[hh, pl.ds(t * grp + g, 1), :]
                                 for g in range(grp)], axis=0)
            m_new = jnp.maximum(m_prev, jnp.max(s, axis=0, keepdims=True))
            alpha = jnp.exp2(m_prev - m_new)
            p = jnp.exp2(s - m_new)
            out += [m_new, alpha * l_prev + jnp.sum(p, axis=0, keepdims=True),
                    alpha * acc + jnp.dot(vtj, p.astype(BF16), preferred_element_type=F32)]
        return tuple(out)

    fin = lax.fori_loop(0, (i + grp - 1) // grp, body, tuple(state))
    halves = [fin[3 * hh + 2][hh * HEAD_DIM:(hh + 1) * HEAD_DIM] / fin[3 * hh + 1] for hh in range(nh)]
    o_ref[0] = jnp.concatenate(halves, axis=0).T


def moba_attention(proj, *, q_col, k_col, v_col):
    bsz, seq, _ = proj.shape
    assert seq % (MOBA_BLOCK * MOBA_BLOCKS_PER_STEP) == 0
    nb = seq // MOBA_BLOCK
    qb, kb, vb = q_col // LANES, k_col // LANES, v_col // LANES
    return pl.pallas_call(
        _moba_kernel,
        grid=(bsz, MOBA_WIDTH // LANES, nb),
        in_specs=[
            pl.BlockSpec((1, MOBA_BLOCK, LANES), lambda b, p, i: (b, i, qb + p)),
            pl.BlockSpec((1, seq, LANES), lambda b, p, i: (b, 0, kb + p)),
            pl.BlockSpec((1, seq, LANES), lambda b, p, i: (b, 0, vb + p)),
        ],
        out_specs=pl.BlockSpec((1, MOBA_BLOCK, LANES), lambda b, p, i: (b, i, p)),
        out_shape=jax.ShapeDtypeStruct((bsz, seq, MOBA_WIDTH), F32),
        scratch_shapes=[pltpu.VMEM((nb, LANES), F32), pltpu.VMEM((seq, LANES), BF16),
                        pltpu.VMEM((LANES, seq), BF16),
                        pltpu.VMEM((LANES // HEAD_DIM, nb, MOBA_BLOCK), F32)],
        compiler_params=_cparams(("parallel", "parallel", "arbitrary")),
        name="moba_attention",
    )(proj, proj, proj)


def _head_sum_matrix(width):
    idx = np.arange(width) // HEAD_DIM
    return jnp.asarray((idx[:, None] == idx[None, :]).astype(np.float32)).astype(BF16)


def _rwkv_prep_kernel(p_ref, prev_ref, mu_ref, w0_ref, wup_ref, a0_ref, aup_ref, gup_ref, kk_ref, ka_ref,
                      rk_ref, hsum_ref, r_out, lw_out, k_out, v_out, kk_out, b_out, g_out, bonus_out):
    s = pl.program_id(1)
    p = p_ref[0]
    ts = p.shape[0]
    row = lax.broadcasted_iota(jnp.int32, p.shape, 0)
    last = jnp.where(s == 0, 0.0, 1.0) * prev_ref[0, SUBLANES - 1:SUBLANES, :]
    prev = jnp.where(row == 0, last, pltpu.roll(p, 1, axis=0))
    p = p + (prev - p) * mu_ref[...]
    w = RWKV_WIDTH
    r, k, v = p[:, 0:w], p[:, w:2 * w], p[:, 2 * w:3 * w]
    lora = p[:, 3 * w:3 * w + LANES]
    gd = p[:, 3 * w + LANES:3 * w + 2 * LANES]
    w_log = -jax.nn.softplus(-(w0_ref[...] + _mm3(jnp.tanh(lora), wup_ref[...]))) - 0.5
    lw_out[0] = -jnp.exp(w_log)
    a = jax.nn.sigmoid(a0_ref[...] + _mm3(lora, aup_ref[...]))
    g_out[0] = _mm(jax.nn.sigmoid(gd), gup_ref[...])
    kk = k * kk_ref[...]
    norm = jnp.sqrt(_mm_exact_rhs(kk * kk, hsum_ref[...]))
    kk = kk / jnp.maximum(norm, 1e-12)
    k = k * (1.0 + (a - 1.0) * ka_ref[...])
    r_out[0] = r
    k_out[0] = k
    v_out[0] = v
    kk_out[0] = kk
    b_out[0] = kk * a
    bonus_out[0] = _mm_exact_rhs(r * k * rk_ref[...], hsum_ref[...]) * v


RWKV_GROUP = 256


def _stack_heads(x):
    lane = lax.broadcasted_iota(jnp.int32, x.shape, 1)
    return jnp.concatenate([jnp.where(lane // HEAD_DIM == hh, x, 0.0)
                            for hh in range(RWKV_GROUP // HEAD_DIM)], axis=0)


def _rwkv_scan_kernel(r_ref, lw_ref, k_ref, v_ref, kk_ref, b_ref, g_ref, bonus_ref, lnw_ref, lnb_ref,
                      havg_ref, o_ref, y_ref, st_ref):
    s = pl.program_id(1)
    ts = r_ref.shape[1]
    cl_ = RWKV_CHUNK
    n = RWKV_GROUP
    ngroups = RWKV_WIDTH // RWKV_GROUP

    @pl.when(s == 0)
    def _():
        st_ref[...] = jnp.zeros_like(st_ref)

    ri = lax.broadcasted_iota(jnp.int32, (n, n), 0)
    ci = lax.broadcasted_iota(jnp.int32, (n, n), 1)
    same = (ri // cl_) == (ci // cl_)
    strict = same & ((ci % cl_) < (ri % cl_))
    incl = same & ((ci % cl_) <= (ri % cl_))
    eye = ri == ci
    tr = lax.broadcasted_iota(jnp.int32, (cl_, cl_), 0)
    tc = lax.broadcasted_iota(jnp.int32, (cl_, cl_), 1)
    cum = jnp.where(tc <= tr, 1.0, 0.0).astype(BF16)

    def chunk(c, _):
        rows = pl.ds(pl.multiple_of(c * cl_, cl_), cl_)
        lw = lw_ref[0, rows, :]
        lw_hi, lw_lo = _split(lw)
        cl_in = (jnp.dot(cum, lw_hi, preferred_element_type=F32)
                 + jnp.dot(cum, lw_lo, preferred_element_type=F32))
        cl_ex = cl_in - lw
        tot = cl_in[cl_ - 1:cl_, :]
        e_in = jnp.exp(cl_in)
        e_ninv = jnp.exp(-cl_in)
        e_rem = jnp.exp(tot - cl_in)
        kk = kk_ref[0, rows, :]
        bb = b_ref[0, rows, :]
        k2 = k_ref[0, rows, :]
        a_t = -kk * jnp.exp(cl_ex)
        b_t = bb * e_ninv
        k_t = k2 * e_ninv
        r_t = r_ref[0, rows, :] * e_in
        b_g = bb * e_rem
        k_g = k2 * e_rem
        vv = v_ref[0, rows, :]
        gam = jnp.exp(tot)
        for grp in range(ngroups):
            ls = slice(grp * n, (grp + 1) * n)
            a_s, b_s, k_s, r_s = (_stack_heads(t[:, ls]) for t in (a_t, b_t, k_t, r_t))
            v_s, bg_s, kg_s = (_stack_heads(t[:, ls]) for t in (vv, b_g, k_g))
            gram = _mm(jnp.concatenate([a_s, r_s], axis=0), jnp.concatenate([b_s, k_s], axis=0), _NT)
            nmat = jnp.where(strict, gram[:n, :n], 0.0)
            m_ak = jnp.where(strict, gram[:n, n:], 0.0)
            m_rb = jnp.where(incl, gram[n:, :n], 0.0)
            m_rk = jnp.where(incl, gram[n:, n:], 0.0)
            winv = jnp.where(eye, 1.0, 0.0) + nmat
            pw = nmat
            for _ in range(int(math.log2(cl_)) - 1):
                pw = _mm(pw, pw)
                winv = winv + _mm(winv, pw)
            st = st_ref[grp]
            ar_st = _mm(jnp.concatenate([a_s, r_s], axis=0), st)
            u_s = _mm(winv, ar_st[:n] + _mm(m_ak, v_s))
            y_s = ar_st[n:] + _mm(m_rb, u_s) + _mm(m_rk, v_s)
            y4 = y_s[0:cl_]
            for hh in range(1, n // cl_):
                y4 = y4 + y_s[hh * cl_:(hh + 1) * cl_]
            y_ref[rows, ls] = y4
            colg = jnp.sum(jnp.where(eye, jnp.broadcast_to(gam[:, ls], (n, n)), 0.0), axis=1, keepdims=True)
            st_ref[grp] = colg * st + _mm(jnp.concatenate([bg_s, kg_s], axis=0),
                                          jnp.concatenate([u_s, v_s], axis=0), _TN)
        return 0

    lax.fori_loop(0, ts // cl_, chunk, 0)

    y = y_ref[...]
    mean = _mm_exact_rhs(y, havg_ref[...])
    yc = y - mean
    var = _mm_exact_rhs(yc * yc, havg_ref[...])
    y = yc * lax.rsqrt(var + RWKV_GN_EPS) * lnw_ref[...] + lnb_ref[...]
    o_ref[0] = (y + bonus_ref[0]) * g_ref[0]


def rwkv7_time_mix(proj, mu, w0, w_up, a0, a_up, g_up, k_k, k_a, r_k, ln_w, ln_b, *, ts):
    bsz, seq, _ = proj.shape
    w = RWKV_WIDTH
    pw = 3 * w + 2 * LANES
    assert RWKV_W_LORA + RWKV_A_LORA == LANES and RWKV_G_LORA == LANES and seq % ts == 0
    wup = jnp.concatenate([w_up, jnp.zeros((RWKV_A_LORA, w), F32)], axis=0)
    aup = jnp.concatenate([jnp.zeros((RWKV_W_LORA, w), F32), a_up], axis=0)
    hsum = _head_sum_matrix(w)
    row = lambda t: t.reshape(1, -1)
    full = lambda shape: pl.BlockSpec(shape, lambda b, s: (0,) * len(shape))
    tile = pl.BlockSpec((1, ts, w), lambda b, s: (b, s, 0))
    nprev = ts // SUBLANES
    outs = pl.pallas_call(
        _rwkv_prep_kernel,
        grid=(bsz, seq // ts),
        in_specs=[
            pl.BlockSpec((1, ts, pw), lambda b, s: (b, s, 0)),
            pl.BlockSpec((1, SUBLANES, pw), lambda b, s: (b, jnp.maximum(s * nprev - 1, 0), 0)),
            full((1, pw)), full((1, w)), full((LANES, w)), full((1, w)), full((LANES, w)),
            full((LANES, w)), full((1, w)), full((1, w)), full((1, w)), full((w, w)),
        ],
        out_specs=[tile] * 8,
        out_shape=[jax.ShapeDtypeStruct((bsz, seq, w), F32)] * 8,
        compiler_params=_cparams(("parallel", "parallel")),
        name="rwkv_prep",
    )(proj, proj, row(mu), row(w0), wup, row(a0), aup, g_up.astype(BF16), row(k_k), row(k_a),
      row(r_k), hsum)
    havg = (_head_sum_matrix(w).astype(F32) / HEAD_DIM).astype(BF16)
    return pl.pallas_call(
        _rwkv_scan_kernel,
        grid=(bsz, seq // ts),
        in_specs=[tile] * 8 + [full((1, w)), full((1, w)), full((w, w))],
        out_specs=tile,
        out_shape=jax.ShapeDtypeStruct((bsz, seq, w), F32),
        scratch_shapes=[pltpu.VMEM((ts, w), F32),
                        pltpu.VMEM((w // RWKV_GROUP, RWKV_GROUP, RWKV_GROUP), F32)],
        compiler_params=_cparams(("parallel", "arbitrary")),
        name="rwkv_scan",
    )(*outs, row(ln_w), row(ln_b), havg)


DSA_KV_TILE = 256
DSA_SEARCH_TILE = 512
DSA_COUNT_ROWS = 64
INT32_MIN = -2 ** 31
NEG_INF_KEY = -2139095041


def _sortable_key(x):
    bits = pltpu.bitcast(x + 0.0, jnp.int32)
    return jnp.where(bits < 0, bits ^ jnp.int32(0x7FFFFFFF), bits)


def _dsa_kernel(q_ref, qi_ref, wi_ref, k_ref, v_ref, ki_ref, o_ref,
                kb_ref, vt_ref, kih_ref, kil_ref, key_ref, acc_ref):
    i = pl.program_id(1)
    nq, kt, st = DSA_QBLOCK, DSA_KV_TILE, DSA_SEARCH_TILE
    seq = k_ref.shape[1]
    topk = min(DSA_TOPK, seq // 4)

    @pl.when(i == 0)
    def _():
        kb_ref[...] = k_ref[0].astype(BF16)
        vrow = lax.broadcasted_iota(jnp.int32, (LANES, seq), 0)
        vt_ref[...] = jnp.where(vrow < HEAD_DIM, v_ref[0].T, 1.0).astype(BF16)
        hi, lo = _split(ki_ref[0])
        kih_ref[...] = hi
        kil_ref[...] = lo

    n_search = (i * nq) // st + 1
    spos = i * nq + lax.broadcasted_iota(jnp.int32, (st, nq), 1)
    lane = lax.broadcasted_iota(jnp.int32, (nq, LANES), 1)
    qpos = i * nq + lax.broadcasted_iota(jnp.int32, (kt, nq), 1)
    krow = lax.broadcasted_iota(jnp.int32, (kt, nq), 0)
    srow = lax.broadcasted_iota(jnp.int32, (st, nq), 0)

    def stack_heads(ref, nheads):
        parts = []
        for h in range(nheads):
            pair = ref[0, :, (h // 2) * LANES:(h // 2 + 1) * LANES]
            parts.append(jnp.where(lane // HEAD_DIM == h % 2, pair, 0.0))
        return jnp.concatenate(parts, axis=0)

    qi_hi, qi_lo = _split(stack_heads(qi_ref, DSA_IDX_HEADS))
    wi_t = (wi_ref[0] * (DSA_IDX_HEADS ** -0.5 * HEAD_DIM ** -0.5)).T
    w_lanes = jnp.concatenate([wi_t[h:h + 1, :] for h in range(DSA_IDX_HEADS)], axis=1)

    def score_tile(c, _):
        rows = pl.ds(pl.multiple_of(c * kt, kt), kt)
        kih, kil = kih_ref[rows, :], kil_ref[rows, :]
        logit = lax.dot_general(kih, qi_hi, _NT, preferred_element_type=F32)
        logit += lax.dot_general(kil, qi_hi, _NT, preferred_element_type=F32)
        logit += lax.dot_general(kih, qi_lo, _NT, preferred_element_type=F32)
        weighted = jnp.maximum(logit, 0.0) * w_lanes
        score = weighted[:, 0:nq]
        for h in range(1, DSA_IDX_HEADS):
            score = score + weighted[:, h * nq:(h + 1) * nq]
        score = jnp.where(c * kt + krow <= qpos, score, -jnp.inf)
        key_ref[rows, :] = _sortable_key(score)
        return 0

    lax.fori_loop(0, n_search * (st // kt), score_tile, 0)

    def count(pred):
        def tile(c, acc):
            rows = pl.ds(pl.multiple_of(c * st, st), st)
            hit = jnp.where(pred(key_ref[rows, :], c * st + srow), 1.0, 0.0)
            for part in range(st // DSA_COUNT_ROWS):
                acc = acc + hit[part * DSA_COUNT_ROWS:(part + 1) * DSA_COUNT_ROWS]
            return acc
        acc = lax.fori_loop(0, n_search, tile, jnp.zeros((DSA_COUNT_ROWS, nq), F32))
        return jnp.sum(acc, axis=0, keepdims=True)

    def search_step(t, base):
        cand = base + lax.shift_left(jnp.int32(1), 31 - t)
        cnt = count(lambda key, pos: key >= cand)
        return jnp.where(cnt >= topk, cand, base)

    thr = lax.fori_loop(0, 32, search_step, jnp.full((1, nq), INT32_MIN, jnp.int32))
    need = topk - count(lambda key, pos: key > thr)
    n_eq = count(lambda key, pos: key == thr)
    tied = jnp.max(jnp.where((n_eq > need) & (thr > NEG_INF_KEY), 1.0, 0.0)) > 0.0
    nbits = seq.bit_length()

    def tie_search(_):
        def step(t, pos):
            cand = pos + lax.shift_left(jnp.int32(1), nbits - 1 - t)
            cnt = count(lambda key, kpos: (key == thr) & (kpos < cand))
            return jnp.where(cnt < need, cand, pos)
        return lax.fori_loop(0, nbits, step, jnp.zeros((1, nq), jnp.int32))

    last = lax.cond(tied, tie_search, lambda _: jnp.full((1, nq), seq, jnp.int32), 0)

    qs = (stack_heads(q_ref, DSA_HEADS) * (HEAD_DIM ** -0.5 * LOG2_E)).astype(BF16)
    acc_ref[...] = jnp.zeros_like(acc_ref)

    def attn_tile(c, m_all):
        rows = pl.ds(pl.multiple_of(c * st, st), st)
        key = key_ref[rows, :]
        kpos = c * st + srow
        chosen = ((key > thr) | ((key == thr) & (kpos <= last))) & (kpos <= spos)
        bias = jnp.where(chosen, 0.0, NEG_BIG)
        kb, vt = kb_ref[rows, :], vt_ref[:, rows]
        m_rows = []
        for h in range(DSA_HEADS):
            s = lax.dot_general(kb, qs[h * nq:(h + 1) * nq], _NT, preferred_element_type=F32) + bias
            m_prev = m_all[h:h + 1, :]
            m_new = jnp.maximum(m_prev, jnp.max(s, axis=0, keepdims=True))
            p = jnp.exp2(s - m_new).astype(BF16)
            acc_ref[h] = jnp.exp2(m_prev - m_new) * acc_ref[h] + jnp.dot(vt, p, preferred_element_type=F32)
            m_rows.append(m_new)
        return jnp.concatenate(m_rows, axis=0)

    lax.fori_loop(0, n_search, attn_tile, jnp.full((DSA_HEADS, nq), NEG_BIG, F32))
    for pr in range(DSA_WIDTH // LANES):
        halves = []
        for h in (2 * pr, 2 * pr + 1):
            acc = acc_ref[h]
            halves.append(acc[:HEAD_DIM] / acc[HEAD_DIM:HEAD_DIM + 1])
        o_ref[0, :, pr * LANES:(pr + 1) * LANES] = jnp.concatenate(halves, axis=0).T


def dsa_attention(proj, *, q_col, qi_col, wi_col, k_col, v_col, ki_col):
    bsz, seq, _ = proj.shape
    assert seq % DSA_SEARCH_TILE == 0 and DSA_SEARCH_TILE % DSA_KV_TILE == 0 and DSA_KV_TILE % DSA_QBLOCK == 0
    nq = seq // DSA_QBLOCK
    kvspec = lambda col: pl.BlockSpec((1, seq, LANES), lambda b, i: (b, 0, col // LANES))
    return pl.pallas_call(
        _dsa_kernel,
        grid=(bsz, nq),
        in_specs=[
            pl.BlockSpec((1, DSA_QBLOCK, DSA_WIDTH), lambda b, i: (b, i, q_col // DSA_WIDTH)),
            pl.BlockSpec((1, DSA_QBLOCK, 2 * LANES), lambda b, i: (b, i, qi_col // (2 * LANES))),
            pl.BlockSpec((1, DSA_QBLOCK, LANES), lambda b, i: (b, i, wi_col // LANES)),
            kvspec(k_col), kvspec(v_col), kvspec(ki_col),
        ],
        out_specs=pl.BlockSpec((1, DSA_QBLOCK, DSA_WIDTH), lambda b, i: (b, i, 0)),
        out_shape=jax.ShapeDtypeStruct((bsz, seq, DSA_WIDTH), F32),
        scratch_shapes=[pltpu.VMEM((seq, LANES), BF16), pltpu.VMEM((LANES, seq), BF16),
                        pltpu.VMEM((seq, LANES), BF16), pltpu.VMEM((seq, LANES), BF16),
                        pltpu.VMEM((seq, DSA_QBLOCK), jnp.int32),
                        pltpu.VMEM((DSA_HEADS, LANES, DSA_QBLOCK), F32)],
        compiler_params=_cparams(("parallel", "arbitrary")),
        name="dsa_attention",
    )(proj, proj, proj, proj, proj, proj)


MOE_TOKEN_TILE = 1024
MOE_ROW_BLOCK = 128


def _router_kernel(h_ref, g_ref, wr_ref, xn_ref, route_ref):
    xn = _rms(h_ref[...], g_ref[...])
    xn_ref[...] = xn.astype(BF16)
    lane = lax.broadcasted_iota(jnp.int32, (xn.shape[0], LANES), 1)
    logits = jnp.where(lane < N_EXPERTS, _mm3(xn, wr_ref[...]), -jnp.inf)
    v1 = jnp.max(logits, axis=1, keepdims=True)
    e1 = jnp.min(jnp.where(logits == v1, lane, LANES), axis=1, keepdims=True)
    rest = jnp.where(lane == e1, -jnp.inf, logits)
    v2 = jnp.max(rest, axis=1, keepdims=True)
    e2 = jnp.min(jnp.where(rest == v2, lane, LANES), axis=1, keepdims=True)
    ratio = jnp.exp(v2 - v1)
    g1 = 1.0 / (1.0 + ratio)
    g2 = ratio * g1
    route_ref[...] = jnp.where(lane == 0, e1.astype(F32), jnp.where(
        lane == 1, e2.astype(F32), jnp.where(lane == 2, g1, jnp.where(lane == 3, g2, 0.0))))


def route_tokens(h2d, g, router, *, tm):
    t, d = h2d.shape
    wr = jnp.zeros((d, LANES), F32).at[:, :N_EXPERTS].set(router)
    return pl.pallas_call(
        _router_kernel,
        grid=(t // tm,),
        in_specs=[pl.BlockSpec((tm, d), lambda i: (i, 0)), pl.BlockSpec((1, d), lambda i: (0, 0)),
                  pl.BlockSpec((d, LANES), lambda i: (0, 0))],
        out_specs=[pl.BlockSpec((tm, d), lambda i: (i, 0)), pl.BlockSpec((tm, LANES), lambda i: (i, 0))],
        out_shape=[jax.ShapeDtypeStruct((t, d), BF16), jax.ShapeDtypeStruct((t, LANES), F32)],
        compiler_params=_cparams(("parallel",)),
        name="moe_router",
    )(h2d, g.reshape(1, d), wr)


def _moe_kernel(cnt_ref, xn_ref, rk_ref, gt_ref, rkt_ref, wg_ref, wu_ref, wd_ref, o_ref, xs_ref, acc_ref):
    tau, e, f = pl.program_id(0), pl.program_id(1), pl.program_id(2)
    tt = xn_ref.shape[0]
    rb = MOE_ROW_BLOCK
    nblk = (cnt_ref[tau * N_EXPERTS + e] + rb - 1) // rb
    slot = lax.broadcasted_iota(jnp.int32, (rb, tt), 0)

    @pl.when((e == 0) & (f == 0))
    def _():
        o_ref[...] = jnp.zeros_like(o_ref)

    def rows_of(b):
        return pl.ds(pl.multiple_of(b * rb, rb), rb)

    @pl.when(f == 0)
    def _():
        rank = rk_ref[0, pl.ds(e, 1), :]

        def gather(b, _):
            onehot = jnp.where(rank == slot + b * rb, 1.0, 0.0).astype(BF16)
            xs_ref[rows_of(b), :] = jnp.dot(onehot, xn_ref[...], preferred_element_type=F32).astype(BF16)
            acc_ref[rows_of(b), :] = jnp.zeros((rb, acc_ref.shape[1]), F32)
            return 0

        lax.fori_loop(0, nblk, gather, 0)

    def ffn(b, _):
        xs = xs_ref[rows_of(b), :]
        gate = jnp.dot(xs, wg_ref[0], preferred_element_type=F32)
        up = jnp.dot(xs, wu_ref[0], preferred_element_type=F32)
        act = (gate * jax.nn.sigmoid(gate) * up).astype(BF16)
        acc_ref[rows_of(b), :] += jnp.dot(act, wd_ref[0], preferred_element_type=F32)
        return 0

    lax.fori_loop(0, nblk, ffn, 0)

    @pl.when(f == pl.num_programs(2) - 1)
    def _():
        rank = rk_ref[0, pl.ds(e, 1), :]
        gates = gt_ref[0, pl.ds(e, 1), :]
        lane_e = lax.broadcasted_iota(jnp.int32, rkt_ref.shape[1:], 1)
        rank_t = jnp.sum(jnp.where(lane_e == e, rkt_ref[0], 0), axis=1, keepdims=True)
        slot_t = lax.broadcasted_iota(jnp.int32, (tt, rb), 1)

        def combine(b, _):
            hit = rank == slot + b * rb
            gate_slot = jnp.sum(jnp.where(hit, gates, 0.0), axis=1, keepdims=True)
            yg = (acc_ref[rows_of(b), :] * gate_slot).astype(BF16)
            scatter = jnp.where(rank_t == slot_t + b * rb, 1.0, 0.0).astype(BF16)
            o_ref[...] += jnp.dot(scatter, yg, preferred_element_type=F32)
            return 0

        lax.fori_loop(0, nblk, combine, 0)


def moe_experts(xn_bf16, route, w_gate, w_up, w_down, *, tf):
    t, d = xn_bf16.shape
    tt = MOE_TOKEN_TILE
    ntile = t // tt
    fdim = w_gate.shape[2]
    experts = route[:, 0:2].astype(jnp.int32).reshape(ntile, tt * 2)
    gates = route[:, 2:4].reshape(ntile, tt * 2)
    onehot = (experts[:, :, None] == jnp.arange(N_EXPERTS)[None, None, :]).astype(jnp.int32)
    csum = jnp.cumsum(onehot, axis=1)
    counts = csum[:, -1, :]
    rank = jnp.where(onehot > 0, csum - 1, -1)
    rank_tok = jnp.max(rank.reshape(ntile, tt, 2, N_EXPERTS), axis=2)
    gate_tok = jnp.sum((onehot * gates[:, :, None]).reshape(ntile, tt, 2, N_EXPERTS), axis=2)
    rk = jnp.swapaxes(rank_tok, 1, 2)
    gt = jnp.swapaxes(gate_tok, 1, 2)
    grid_spec = pltpu.PrefetchScalarGridSpec(
        num_scalar_prefetch=1,
        grid=(ntile, N_EXPERTS, fdim // tf),
        in_specs=[
            pl.BlockSpec((tt, d), lambda i, e, f, c: (i, 0)),
            pl.BlockSpec((1, N_EXPERTS, tt), lambda i, e, f, c: (i, 0, 0)),
            pl.BlockSpec((1, N_EXPERTS, tt), lambda i, e, f, c: (i, 0, 0)),
            pl.BlockSpec((1, tt, N_EXPERTS), lambda i, e, f, c: (i, 0, 0)),
            pl.BlockSpec((1, d, tf), lambda i, e, f, c: (e, 0, f)),
            pl.BlockSpec((1, d, tf), lambda i, e, f, c: (e, 0, f)),
            pl.BlockSpec((1, tf, d), lambda i, e, f, c: (e, f, 0)),
        ],
        out_specs=pl.BlockSpec((tt, d), lambda i, e, f, c: (i, 0)),
        scratch_shapes=[pltpu.VMEM((tt, d), BF16), pltpu.VMEM((tt, d), F32)],
    )
    return pl.pallas_call(
        _moe_kernel,
        grid_spec=grid_spec,
        out_shape=jax.ShapeDtypeStruct((t, d), F32),
        compiler_params=_cparams(("parallel", "arbitrary", "arbitrary")),
        name="moe_experts",
    )(counts.reshape(-1), xn_bf16, rk, gt, rank_tok, w_gate, w_up, w_down)


def _add_norm_kernel(a_ref, b_ref, g_ref, o_ref, *, normalize):
    y = a_ref[...] + b_ref[...]
    o_ref[...] = _rms(y, g_ref[...]) if normalize else y


def add_rmsnorm(a, b, g, *, tm):
    t, d = a.shape
    spec = pl.BlockSpec((tm, d), lambda i: (i, 0))
    gain = jnp.ones((1, d), F32) if g is None else g.reshape(1, d)
    return pl.pallas_call(
        functools.partial(_add_norm_kernel, normalize=g is not None),
        grid=(t // tm,),
        in_specs=[spec, spec, pl.BlockSpec((1, d), lambda i: (0, 0))],
        out_specs=spec,
        out_shape=jax.ShapeDtypeStruct((t, d), F32),
        compiler_params=_cparams(("parallel",)),
        name="add_rmsnorm",
    )(a, b, gain)


ROW_TILE = 512

EVEN_Q_COL, EVEN_K_COL, EVEN_V_COL = 512, 1024, 1536
EVEN_ROPE_COLS = np.zeros(2048, bool)
EVEN_ROPE_COLS[EVEN_Q_COL:EVEN_V_COL] = True


def _odd_layout():
    rw = 3 * RWKV_WIDTH + RWKV_W_LORA + RWKV_A_LORA + RWKV_G_LORA
    q0 = rw
    k0 = q0 + DSA_WIDTH
    v0 = k0 + HEAD_DIM
    qi0 = v0 + HEAD_DIM
    ki0 = qi0 + DSA_IDX_HEADS * HEAD_DIM
    wi0 = ki0 + HEAD_DIM
    seg = lambda start, n: list(range(start, start + n))
    src, keep, rope = [], [], []

    def add(cols, roped, pad=0):
        src.extend(cols + [0] * pad)
        keep.extend([1.0] * len(cols) + [0.0] * pad)
        rope.extend([roped] * (len(cols) + pad))

    add(seg(0, rw), False)
    cols = {"k_col": len(src)}
    add(seg(k0, HEAD_DIM) * 2, True)
    cols["v_col"] = len(src)
    add(seg(v0, HEAD_DIM) * 2, False)
    cols["q_col"] = len(src)
    add(seg(q0, DSA_WIDTH), True)
    cols["qi_col"] = len(src)
    add(seg(qi0, DSA_IDX_HEADS * HEAD_DIM), True)
    cols["ki_col"] = len(src)
    add(seg(ki0, HEAD_DIM) * 2, True)
    cols["wi_col"] = len(src)
    add(seg(wi0, DSA_IDX_HEADS), False, pad=LANES - DSA_IDX_HEADS)
    return (np.asarray(src, np.int32), np.asarray(keep, np.float32), np.asarray(rope, bool), cols)


ODD_COL_SRC, ODD_COL_KEEP, ODD_ROPE_COLS, ODD_DSA_COLS = _odd_layout()


def kernel(x, e_norm_mix, e_w_in, s5_a_re, s5_a_im, s5_log_dt, s5_b_re, s5_b_im, s5_c_re, s5_c_im, s5_d, s5_glu_w, s5_glu_b, e_w_out, e_norm_ffn, ffn_w_gate, ffn_w_up, ffn_w_down, o_norm_mix, o_w_in, rwkv_mu, rwkv_w0, rwkv_w_up, rwkv_a0, rwkv_a_up, rwkv_g_up, rwkv_k_k, rwkv_k_a, rwkv_r_k, rwkv_ln_w, rwkv_ln_b, o_w_out, o_norm_ffn, moe_router, moe_w_gate, moe_w_up, moe_w_down, final_norm):
    bsz, seq, d = x.shape
    n_even, n_odd = e_norm_mix.shape[0], o_norm_mix.shape[0]
    h = x.reshape(-1, d)
    zero = jnp.zeros((bsz * seq, d), F32)
    for layer in range(n_even + n_odd):
        j = layer // 2
        last = layer == n_even + n_odd - 1
        if layer % 2 == 0:
            proj = norm_matmul_rope(h, e_norm_mix[j], e_w_in[j].astype(BF16), EVEN_ROPE_COLS, seq,
                                    tm=ROW_TILE, tn=512).reshape(bsz, seq, -1)
            ya = s5_mixer(proj, s5_a_re[j], s5_a_im[j], s5_log_dt[j], s5_b_re[j], s5_b_im[j], s5_c_re[j],
                          s5_c_im[j], s5_d[j], s5_glu_w[j], s5_glu_b[j], ts=256)
            yb = moba_attention(proj, q_col=EVEN_Q_COL, k_col=EVEN_K_COL, v_col=EVEN_V_COL)
            h = out_proj_residual(h, ya.reshape(-1, S5_WIDTH), yb.reshape(-1, MOBA_WIDTH),
                                  e_w_out[j].astype(BF16), tm=ROW_TILE)
            h = dense_ffn_residual(h, e_norm_ffn[j], ffn_w_gate[j].astype(BF16), ffn_w_up[j].astype(BF16),
                                   ffn_w_down[j].astype(BF16), tm=ROW_TILE, tf=256)
            if last:
                h = add_rmsnorm(h, zero, final_norm, tm=ROW_TILE)
        else:
            w_in = jnp.take(o_w_in[j], jnp.asarray(ODD_COL_SRC), axis=1) * jnp.asarray(ODD_COL_KEEP)[None, :]
            proj = norm_matmul_rope(h, o_norm_mix[j], w_in.astype(BF16), ODD_ROPE_COLS, seq,
                                    tm=ROW_TILE, tn=512).reshape(bsz, seq, -1)
            yc = rwkv7_time_mix(proj, rwkv_mu[j], rwkv_w0[j], rwkv_w_up[j], rwkv_a0[j], rwkv_a_up[j],
                                rwkv_g_up[j], rwkv_k_k[j], rwkv_k_a[j], rwkv_r_k[j], rwkv_ln_w[j],
                                rwkv_ln_b[j], ts=256)
            yd = dsa_attention(proj, **ODD_DSA_COLS)
            h = out_proj_residual(h, yc.reshape(-1, RWKV_WIDTH), yd.reshape(-1, DSA_WIDTH),
                                  o_w_out[j].astype(BF16), tm=ROW_TILE)
            xn, route = route_tokens(h, o_norm_ffn[j], moe_router[j], tm=ROW_TILE)
            y = moe_experts(xn, route, moe_w_gate[j].astype(BF16), moe_w_up[j].astype(BF16),
                            moe_w_down[j].astype(BF16), tf=512)
            h = add_rmsnorm(h, y, final_norm if last else None, tm=ROW_TILE)
    return h.reshape(bsz, seq, d)
```

```python
import functools
import math

import jax
import jax.numpy as jnp
import numpy as np
from jax import lax
from jax.experimental import pallas as pl
from jax.experimental.pallas import tpu as pltpu

F32 = jnp.float32
BF16 = jnp.bfloat16
HIGHEST = lax.Precision.HIGHEST

LANES = 128
SUBLANES = 8
VMEM_LIMIT_BYTES = 56 * 1024 * 1024

D_MODEL = 1024
HEAD_DIM = 64
ROT_DIM = HEAD_DIM // 4
ROPE_THETA = 500000.0
NORM_EPS = 1e-6
S5_WIDTH = 512
S5_GROUP = 16
S5_GROUPS = S5_WIDTH // S5_GROUP
S5_STATE = 64
S5_NSTATE = S5_GROUPS * S5_STATE
MOBA_WIDTH = 512
MOBA_HEADS = MOBA_WIDTH // HEAD_DIM
MOBA_BLOCK = 256
MOBA_TOPK = 3
MOBA_BLOCKS_PER_STEP = 4
RWKV_WIDTH = 512
RWKV_HEADS = RWKV_WIDTH // HEAD_DIM
RWKV_W_LORA = 64
RWKV_A_LORA = 64
RWKV_G_LORA = 128
RWKV_GN_EPS = 64e-5
RWKV_CHUNK = 64
DSA_WIDTH = 512
DSA_HEADS = DSA_WIDTH // HEAD_DIM
DSA_IDX_HEADS = 4
DSA_TOPK = 256
DSA_QBLOCK = 128
N_EXPERTS = 8
NEG_BIG = -1e30
LOG2_E = 1.4426950408889634


def _cparams(sem):
    return pltpu.CompilerParams(dimension_semantics=sem, vmem_limit_bytes=VMEM_LIMIT_BYTES)


def _rms(x, g):
    return x * lax.rsqrt(jnp.mean(x * x, axis=-1, keepdims=True) + NORM_EPS) * g


def _norm_matmul_kernel(flags_ref, h_ref, g_ref, w_ref, cos_ref, sina_ref, sinb_ref, colf_ref,
                        o_ref, xn_ref):
    j = pl.program_id(1)

    @pl.when(j == 0)
    def _():
        xn_ref[...] = _rms(h_ref[...], g_ref[...]).astype(BF16)

    acc = jnp.dot(xn_ref[...], w_ref[...], preferred_element_type=F32)
    tn = acc.shape[1]

    @pl.when(flags_ref[j] == 0)
    def _():
        o_ref[...] = acc

    @pl.when(flags_ref[j] != 0)
    def _():
        reps = tn // LANES
        colf = colf_ref[...]
        cos = 1.0 + colf * (jnp.tile(cos_ref[...], (1, reps)) - 1.0)
        sina = colf * jnp.tile(sina_ref[...], (1, reps))
        sinb = colf * jnp.tile(sinb_ref[...], (1, reps))
        nxt = pltpu.roll(acc, tn - ROT_DIM // 2, axis=1)
        prv = pltpu.roll(acc, ROT_DIM // 2, axis=1)
        o_ref[...] = acc * cos + nxt * sina + prv * sinb


def _rope_tables(seq):
    half = ROT_DIM // 2
    inv_freq = ROPE_THETA ** (-jnp.arange(half, dtype=F32) / half)
    ang = jnp.arange(seq, dtype=F32)[:, None] * inv_freq[None, :]
    cos, sin = jnp.cos(ang), jnp.sin(ang)
    ones = jnp.ones((seq, HEAD_DIM - ROT_DIM), F32)
    zeros = jnp.zeros((seq, HEAD_DIM - ROT_DIM), F32)
    zh = jnp.zeros((seq, half), F32)
    cos64 = jnp.concatenate([cos, cos, ones], axis=1)
    sina64 = jnp.concatenate([-sin, zh, zeros], axis=1)
    sinb64 = jnp.concatenate([zh, sin, zeros], axis=1)
    rep = LANES // HEAD_DIM
    return jnp.tile(cos64, (1, rep)), jnp.tile(sina64, (1, rep)), jnp.tile(sinb64, (1, rep))


def norm_matmul_rope(h2d, g, w_bf16, rope_cols, seq, *, tm, tn):
    t, d = h2d.shape
    n = w_bf16.shape[1]
    assert t % tm == 0 and n % tn == 0 and seq % tm == 0 and tn % LANES == 0
    nseq = seq // tm
    cos, sina, sinb = _rope_tables(seq)
    tile_flags = jnp.asarray(rope_cols.reshape(n // tn, tn).any(axis=1).astype(np.int32))
    colf = jnp.asarray(rope_cols.astype(np.float32))[None, :]
    grid_spec = pltpu.PrefetchScalarGridSpec(
        num_scalar_prefetch=1,
        grid=(t // tm, n // tn),
        in_specs=[
            pl.BlockSpec((tm, d), lambda i, j, f: (i, 0)),
            pl.BlockSpec((1, d), lambda i, j, f: (0, 0)),
            pl.BlockSpec((d, tn), lambda i, j, f: (0, j)),
            pl.BlockSpec((tm, LANES), lambda i, j, f: (i % nseq, 0)),
            pl.BlockSpec((tm, LANES), lambda i, j, f: (i % nseq, 0)),
            pl.BlockSpec((tm, LANES), lambda i, j, f: (i % nseq, 0)),
            pl.BlockSpec((1, tn), lambda i, j, f: (0, j)),
        ],
        out_specs=pl.BlockSpec((tm, tn), lambda i, j, f: (i, j)),
        scratch_shapes=[pltpu.VMEM((tm, d), BF16)],
    )
    return pl.pallas_call(
        _norm_matmul_kernel,
        grid_spec=grid_spec,
        out_shape=jax.ShapeDtypeStruct((t, n), F32),
        compiler_params=_cparams(("parallel", "arbitrary")),
        name="norm_matmul_rope",
    )(tile_flags, h2d, g.reshape(1, d), w_bf16, cos, sina, sinb, colf)


def _out_proj_kernel(res_ref, a_ref, b_ref, wa_ref, wb_ref, o_ref):
    acc = jnp.dot(a_ref[...].astype(BF16), wa_ref[...], preferred_element_type=F32)
    acc += jnp.dot(b_ref[...].astype(BF16), wb_ref[...], preferred_element_type=F32)
    o_ref[...] = res_ref[...] + acc


def out_proj_residual(res, a, b, w_bf16, *, tm):
    t, d = res.shape
    ka, kb = a.shape[1], b.shape[1]
    wa, wb = w_bf16[:ka], w_bf16[ka:]
    return pl.pallas_call(
        _out_proj_kernel,
        grid=(t // tm,),
        in_specs=[
            pl.BlockSpec((tm, d), lambda i: (i, 0)),
            pl.BlockSpec((tm, ka), lambda i: (i, 0)),
            pl.BlockSpec((tm, kb), lambda i: (i, 0)),
            pl.BlockSpec((ka, d), lambda i: (0, 0)),
            pl.BlockSpec((kb, d), lambda i: (0, 0)),
        ],
        out_specs=pl.BlockSpec((tm, d), lambda i: (i, 0)),
        out_shape=jax.ShapeDtypeStruct((t, d), F32),
        compiler_params=_cparams(("parallel",)),
        name="out_proj_residual",
    )(res, a, b, wa, wb)


def _dense_ffn_kernel(h_ref, g_ref, wg_ref, wu_ref, wd_ref, o_ref, xn_ref, acc_ref):
    f = pl.program_id(1)

    @pl.when(f == 0)
    def _():
        xn_ref[...] = _rms(h_ref[...], g_ref[...]).astype(BF16)
        acc_ref[...] = jnp.zeros_like(acc_ref)

    xn = xn_ref[...]
    gate = jnp.dot(xn, wg_ref[...], preferred_element_type=F32)
    up = jnp.dot(xn, wu_ref[...], preferred_element_type=F32)
    act = (gate * jax.nn.sigmoid(gate) * up).astype(BF16)
    acc_ref[...] += jnp.dot(act, wd_ref[...], preferred_element_type=F32)

    @pl.when(f == pl.num_programs(1) - 1)
    def _():
        o_ref[...] = h_ref[...] + acc_ref[...]


def dense_ffn_residual(h2d, g, wg, wu, wd, *, tm, tf):
    t, d = h2d.shape
    fdim = wg.shape[1]
    assert t % tm == 0 and fdim % tf == 0
    return pl.pallas_call(
        _dense_ffn_kernel,
        grid=(t // tm, fdim // tf),
        in_specs=[
            pl.BlockSpec((tm, d), lambda i, f: (i, 0)),
            pl.BlockSpec((1, d), lambda i, f: (0, 0)),
            pl.BlockSpec((d, tf), lambda i, f: (0, f)),
            pl.BlockSpec((d, tf), lambda i, f: (0, f)),
            pl.BlockSpec((tf, d), lambda i, f: (f, 0)),
        ],
        out_specs=pl.BlockSpec((tm, d), lambda i, f: (i, 0)),
        out_shape=jax.ShapeDtypeStruct((t, d), F32),
        scratch_shapes=[pltpu.VMEM((tm, d), BF16), pltpu.VMEM((tm, d), F32)],
        compiler_params=_cparams(("parallel", "arbitrary")),
        name="dense_ffn_residual",
    )(h2d, g.reshape(1, d), wg, wu, wd)


S5_LANE_CHUNK = 512
S5_MXU_TILE = 256


def _s5_kernel(u_ref, bre_ref, bim_ref, cre_ref, cim_ref, stepr_ref, stepi_ref, carr_ref, cari_ref,
               d_ref, gw_ref, gb_ref, o_ref, xr_ref, xi_ref, sr_ref, si_ref):
    s = pl.program_id(1)
    ts = u_ref.shape[1]

    @pl.when(s == 0)
    def _():
        sr_ref[...] = jnp.zeros_like(sr_ref)
        si_ref[...] = jnp.zeros_like(si_ref)

    u = u_ref[0]
    ub = u.astype(BF16)
    ratio = S5_NSTATE // S5_WIDTH
    for n in range(S5_NSTATE // S5_MXU_TILE):
        cols = slice(n * S5_MXU_TILE, (n + 1) * S5_MXU_TILE)
        chans = slice((n // ratio) * S5_MXU_TILE, (n // ratio + 1) * S5_MXU_TILE)
        xr_ref[:, cols] = jnp.dot(ub[:, chans], bre_ref[chans, cols], preferred_element_type=F32)
        xi_ref[:, cols] = jnp.dot(ub[:, chans], bim_ref[chans, cols], preferred_element_type=F32)

    for c in range(S5_NSTATE // S5_LANE_CHUNK):
        lanes = pl.ds(c * S5_LANE_CHUNK, S5_LANE_CHUNK)
        steps = [(stepr_ref[k, :, lanes], stepi_ref[k, :, lanes]) for k in range(3)]
        pr, pi = carr_ref[:, lanes], cari_ref[:, lanes]

        def body(g, carry, lanes=lanes, steps=steps, pr=pr, pi=pi):
            cr, ci = carry
            rows = pl.ds(pl.multiple_of(g * SUBLANES, SUBLANES), SUBLANES)
            xr = xr_ref[rows, lanes]
            xi = xi_ref[rows, lanes]
            for k, (ar, ai) in enumerate(steps):
                rr = pltpu.roll(xr, 1 << k, axis=0)
                ri = pltpu.roll(xi, 1 << k, axis=0)
                xr, xi = xr + ar * rr - ai * ri, xi + ar * ri + ai * rr
            xr = xr + pr * cr - pi * ci
            xi = xi + pr * ci + pi * cr
            xr_ref[rows, lanes] = xr
            xi_ref[rows, lanes] = xi
            return xr[SUBLANES - 1:SUBLANES, :], xi[SUBLANES - 1:SUBLANES, :]

        cr, ci = lax.fori_loop(0, ts // SUBLANES, body, (sr_ref[:, lanes], si_ref[:, lanes]))
        sr_ref[:, lanes] = cr
        si_ref[:, lanes] = ci

    y_tiles = []
    for m in range(S5_WIDTH // S5_MXU_TILE):
        chans = slice(m * S5_MXU_TILE, (m + 1) * S5_MXU_TILE)
        states = slice(m * ratio * S5_MXU_TILE, (m + 1) * ratio * S5_MXU_TILE)
        y_tiles.append(
            jnp.dot(xr_ref[:, states].astype(BF16), cre_ref[states, chans], preferred_element_type=F32)
            - jnp.dot(xi_ref[:, states].astype(BF16), cim_ref[states, chans], preferred_element_type=F32))
    y = jnp.concatenate(y_tiles, axis=1)
    y = jax.nn.gelu(y + d_ref[...] * u)
    z = jnp.dot(y.astype(BF16), gw_ref[...], preferred_element_type=F32) + gb_ref[...]
    o_ref[0] = y * jax.nn.sigmoid(z)


def _s5_params(a_re, a_im, log_dt, b_re, b_im, c_re, c_im):
    dt = jnp.exp(log_dt)[:, None]

    def apow(n):
        mag = jnp.exp(n * dt * a_re)
        return mag * jnp.cos(n * dt * a_im), mag * jnp.sin(n * dt * a_im)

    abar_re, abar_im = apow(1.0)
    den = a_re * a_re + a_im * a_im
    xr, xi = abar_re - 1.0, abar_im
    f_re = (xr * a_re + xi * a_im) / den
    f_im = (xi * a_re - xr * a_im) / den
    bbar_re = f_re[..., None] * b_re - f_im[..., None] * b_im
    bbar_im = f_re[..., None] * b_im + f_im[..., None] * b_re
    eye = jnp.eye(S5_GROUPS, dtype=F32)

    def blockdiag_in(b):
        return jnp.einsum('gpc,gh->gchp', b, eye).reshape(S5_WIDTH, S5_NSTATE)

    def blockdiag_out(c):
        return jnp.einsum('gcp,gh->gphc', c, eye).reshape(S5_NSTATE, S5_WIDTH)

    row = jnp.arange(SUBLANES)[:, None]
    step_r, step_i = [], []
    for k in range(3):
        pr, pi = apow(float(1 << k))
        keep = row >= (1 << k)
        step_r.append(jnp.where(keep, pr.reshape(1, -1), 0.0))
        step_i.append(jnp.where(keep, pi.reshape(1, -1), 0.0))
    car = [apow(float(r + 1)) for r in range(SUBLANES)]
    car_r = jnp.stack([p[0].reshape(-1) for p in car])
    car_i = jnp.stack([p[1].reshape(-1) for p in car])
    return (blockdiag_in(bbar_re).astype(BF16), blockdiag_in(bbar_im).astype(BF16),
            blockdiag_out(c_re).astype(BF16), blockdiag_out(c_im).astype(BF16),
            jnp.stack(step_r), jnp.stack(step_i), car_r, car_i)


def s5_mixer(proj, a_re, a_im, log_dt, b_re, b_im, c_re, c_im, d, glu_w, glu_b, *, ts):
    bsz, seq, _ = proj.shape
    bre, bim, cre, cim, step_r, step_i, car_r, car_i = _s5_params(a_re, a_im, log_dt, b_re, b_im, c_re, c_im)
    full = lambda shape: pl.BlockSpec(shape, lambda b, s: (0,) * len(shape))
    return pl.pallas_call(
        _s5_kernel,
        grid=(bsz, seq // ts),
        in_specs=[
            pl.BlockSpec((1, ts, S5_WIDTH), lambda b, s: (b, s, 0)),
            full((S5_WIDTH, S5_NSTATE)), full((S5_WIDTH, S5_NSTATE)),
            full((S5_NSTATE, S5_WIDTH)), full((S5_NSTATE, S5_WIDTH)),
            full((3, SUBLANES, S5_NSTATE)), full((3, SUBLANES, S5_NSTATE)),
            full((SUBLANES, S5_NSTATE)), full((SUBLANES, S5_NSTATE)),
            full((1, S5_WIDTH)), full((S5_WIDTH, S5_WIDTH)), full((1, S5_WIDTH)),
        ],
        out_specs=pl.BlockSpec((1, ts, S5_WIDTH), lambda b, s: (b, s, 0)),
        out_shape=jax.ShapeDtypeStruct((bsz, seq, S5_WIDTH), F32),
        scratch_shapes=[pltpu.VMEM((ts, S5_NSTATE), F32), pltpu.VMEM((ts, S5_NSTATE), F32),
                        pltpu.VMEM((1, S5_NSTATE), F32), pltpu.VMEM((1, S5_NSTATE), F32)],
        compiler_params=_cparams(("parallel", "arbitrary")),
        name="s5_mixer",
    )(proj, bre, bim, cre, cim, step_r, step_i, car_r, car_i,
      d.reshape(1, -1), glu_w.astype(BF16), glu_b.reshape(1, -1))


_NT = (((1,), (1,)), ((), ()))
_NN = (((1,), (0,)), ((), ()))
_TN = (((0,), (0,)), ((), ()))


def _split(x):
    hi = x.astype(BF16)
    return hi, (x - hi.astype(F32)).astype(BF16)


def _mm(a, b, dims=_NN):
    return lax.dot_general(a.astype(BF16), b.astype(BF16), dims, preferred_element_type=F32)


def _mm3(a, b, dims=_NN):
    ah, al = _split(a)
    bh, bl = _split(b)
    out = lax.dot_general(ah, bh, dims, preferred_element_type=F32)
    out += lax.dot_general(ah, bl, dims, preferred_element_type=F32)
    out += lax.dot_general(al, bh, dims, preferred_element_type=F32)
    return out


def _mm_exact_rhs(a, b01, dims=_NN):
    ah, al = _split(a)
    out = lax.dot_general(ah, b01, dims, preferred_element_type=F32)
    out += lax.dot_general(al, b01, dims, preferred_element_type=F32)
    return out


def _moba_kernel(q_ref, k_ref, v_ref, o_ref, kmean_ref, kb_ref, vt_ref, bias_ref):
    i = pl.program_id(2)
    blk = MOBA_BLOCK
    seq = k_ref.shape[1]
    nb = seq // blk
    nh = LANES // HEAD_DIM

    @pl.when(i == 0)
    def _():
        k = k_ref[0]
        kmean_ref[...] = jnp.mean(k.reshape(nb, blk, LANES), axis=1)
        kb_ref[...] = k.astype(BF16)
        vt_ref[...] = v_ref[0].T.astype(BF16)

    q = q_ref[0]
    lane = lax.broadcasted_iota(jnp.int32, (blk, LANES), 1)
    brow = lax.broadcasted_iota(jnp.int32, (nb, blk), 0)
    krow = lax.broadcasted_iota(jnp.int32, (blk, blk), 0)
    qcol = lax.broadcasted_iota(jnp.int32, (blk, blk), 1)
    own = pl.ds(pl.multiple_of(i * blk, blk), blk)
    k_own, vt_own = kb_ref[own, :], vt_ref[:, own]
    qbs, state = [], []
    for hh in range(nh):
        qh = jnp.where(lane // HEAD_DIM == hh, q, 0.0)
        work = jnp.where(brow < i, _mm3(kmean_ref[...], qh, _NT), -jnp.inf)
        sel = jnp.zeros((nb, blk), F32)
        for _ in range(MOBA_TOPK):
            m = jnp.max(work, axis=0, keepdims=True)
            first = jnp.min(jnp.where(work == m, brow, nb), axis=0, keepdims=True)
            pick = (brow == first) & (m > -jnp.inf)
            sel = jnp.where(pick, 1.0, sel)
            work = jnp.where(pick, -jnp.inf, work)
        bias_ref[hh] = jnp.where(sel > 0.0, 0.0, NEG_BIG)

        qb = (qh * (HEAD_DIM ** -0.5 * LOG2_E)).astype(BF16)
        s = lax.dot_general(k_own, qb, _NT, preferred_element_type=F32)
        s = jnp.where(krow <= qcol, s, NEG_BIG)
        m0 = jnp.max(s, axis=0, keepdims=True)
        p = jnp.exp2(s - m0)
        acc0 = jnp.dot(vt_own, p.astype(BF16), preferred_element_type=F32)
        qbs.append(qb)
        state += [m0, jnp.sum(p, axis=0, keepdims=True), acc0]

    grp = MOBA_BLOCKS_PER_STEP

    def body(t, carry):
        rows = pl.ds(pl.multiple_of(t * (grp * blk), grp * blk), grp * blk)
        kj, vtj = kb_ref[rows, :], vt_ref[:, rows]
        out = []
        for hh in range(nh):
            m_prev, l_prev, acc = carry[3 * hh:3 * hh + 3]
            s = lax.dot_general(kj, qbs[hh], _NT, preferred_element_type=F32)
            s = jnp.concatenate([s[g * blk:(g + 1) * blk] + bias_ref[hh, pl.ds(t * grp + g, 1), :]
                                 for g in range(grp)], axis=0)
            m_new = jnp.maximum(m_prev, jnp.max(s, axis=0, keepdims=True))
            alpha = jnp.exp2(m_prev - m_new)
            p = jnp.exp2(s - m_new)
            out += [m_new, alpha * l_prev + jnp.sum(p, axis=0, keepdims=True),
                    alpha * acc + jnp.dot(vtj, p.astype(BF16), preferred_element_type=F32)]
        return tuple(out)

    fin = lax.fori_loop(0, (i + grp - 1) // grp, body, tuple(state))
    halves = [fin[3 * hh + 2][hh * HEAD_DIM:(hh + 1) * HEAD_DIM] / fin[3 * hh + 1] for hh in range(nh)]
    o_ref[0] = jnp.concatenate(halves, axis=0).T


def moba_attention(proj, *, q_col, k_col, v_col):
    bsz, seq, _ = proj.shape
    assert seq % (MOBA_BLOCK * MOBA_BLOCKS_PER_STEP) == 0
    nb = seq // MOBA_BLOCK
    qb, kb, vb = q_col // LANES, k_col // LANES, v_col // LANES
    return pl.pallas_call(
        _moba_kernel,
        grid=(bsz, MOBA_WIDTH // LANES, nb),
        in_specs=[
            pl.BlockSpec((1, MOBA_BLOCK, LANES), lambda b, p, i: (b, i, qb + p)),
            pl.BlockSpec((1, seq, LANES), lambda b, p, i: (b, 0, kb + p)),
            pl.BlockSpec((1, seq, LANES), lambda b, p, i: (b, 0, vb + p)),
        ],
        out_specs=pl.BlockSpec((1, MOBA_BLOCK, LANES), lambda b, p, i: (b, i, p)),
        out_shape=jax.ShapeDtypeStruct((bsz, seq, MOBA_WIDTH), F32),
        scratch_shapes=[pltpu.VMEM((nb, LANES), F32), pltpu.VMEM((seq, LANES), BF16),
                        pltpu.VMEM((LANES, seq), BF16),
                        pltpu.VMEM((LANES // HEAD_DIM, nb, MOBA_BLOCK), F32)],
        compiler_params=_cparams(("parallel", "parallel", "arbitrary")),
        name="moba_attention",
    )(proj, proj, proj)


def _head_sum_matrix(width):
    idx = np.arange(width) // HEAD_DIM
    return jnp.asarray((idx[:, None] == idx[None, :]).astype(np.float32)).astype(BF16)


def _rwkv_prep_kernel(p_ref, prev_ref, mu_ref, w0_ref, wup_ref, a0_ref, aup_ref, gup_ref, kk_ref, ka_ref,
                      rk_ref, hsum_ref, r_out, lw_out, k_out, v_out, kk_out, b_out, g_out, bonus_out):
    s = pl.program_id(1)
    p = p_ref[0]
    ts = p.shape[0]
    row = lax.broadcasted_iota(jnp.int32, p.shape, 0)
    last = jnp.where(s == 0, 0.0, 1.0) * prev_ref[0, SUBLANES - 1:SUBLANES, :]
    prev = jnp.where(row == 0, last, pltpu.roll(p, 1, axis=0))
    p = p + (prev - p) * mu_ref[...]
    w = RWKV_WIDTH
    r, k, v = p[:, 0:w], p[:, w:2 * w], p[:, 2 * w:3 * w]
    lora = p[:, 3 * w:3 * w + LANES]
    gd = p[:, 3 * w + LANES:3 * w + 2 * LANES]
    w_log = -jax.nn.softplus(-(w0_ref[...] + _mm3(jnp.tanh(lora), wup_ref[...]))) - 0.5
    lw_out[0] = -jnp.exp(w_log)
    a = jax.nn.sigmoid(a0_ref[...] + _mm3(lora, aup_ref[...]))
    g_out[0] = _mm(jax.nn.sigmoid(gd), gup_ref[...])
    kk = k * kk_ref[...]
    norm = jnp.sqrt(_mm_exact_rhs(kk * kk, hsum_ref[...]))
    kk = kk / jnp.maximum(norm, 1e-12)
    k = k * (1.0 + (a - 1.0) * ka_ref[...])
    r_out[0] = r
    k_out[0] = k
    v_out[0] = v
    kk_out[0] = kk
    b_out[0] = kk * a
    bonus_out[0] = _mm_exact_rhs(r * k * rk_ref[...], hsum_ref[...]) * v


RWKV_GROUP = 256


def _stack_heads(x):
    lane = lax.broadcasted_iota(jnp.int32, x.shape, 1)
    return jnp.concatenate([jnp.where(lane // HEAD_DIM == hh, x, 0.0)
                            for hh in range(RWKV_GROUP // HEAD_DIM)], axis=0)


def _rwkv_scan_kernel(r_ref, lw_ref, k_ref, v_ref, kk_ref, b_ref, g_ref, bonus_ref, lnw_ref, lnb_ref,
                      havg_ref, o_ref, y_ref, st_ref):
    s = pl.program_id(1)
    ts = r_ref.shape[1]
    cl_ = RWKV_CHUNK
    n = RWKV_GROUP
    ngroups = RWKV_WIDTH // RWKV_GROUP

    @pl.when(s == 0)
    def _():
        st_ref[...] = jnp.zeros_like(st_ref)

    ri = lax.broadcasted_iota(jnp.int32, (n, n), 0)
    ci = lax.broadcasted_iota(jnp.int32, (n, n), 1)
    same = (ri // cl_) == (ci // cl_)
    strict = same & ((ci % cl_) < (ri % cl_))
    incl = same & ((ci % cl_) <= (ri % cl_))
    eye = ri == ci
    tr = lax.broadcasted_iota(jnp.int32, (cl_, cl_), 0)
    tc = lax.broadcasted_iota(jnp.int32, (cl_, cl_), 1)
    cum = jnp.where(tc <= tr, 1.0, 0.0).astype(BF16)

    def chunk(c, _):
        rows = pl.ds(pl.multiple_of(c * cl_, cl_), cl_)
        lw = lw_ref[0, rows, :]
        lw_hi, lw_lo = _split(lw)
        cl_in = (jnp.dot(cum, lw_hi, preferred_element_type=F32)
                 + jnp.dot(cum, lw_lo, preferred_element_type=F32))
        cl_ex = cl_in - lw
        tot = cl_in[cl_ - 1:cl_, :]
        e_in = jnp.exp(cl_in)
        e_ninv = jnp.exp(-cl_in)
        e_rem = jnp.exp(tot - cl_in)
        kk = kk_ref[0, rows, :]
        bb = b_ref[0, rows, :]
        k2 = k_ref[0, rows, :]
        a_t = -kk * jnp.exp(cl_ex)
        b_t = bb * e_ninv
        k_t = k2 * e_ninv
        r_t = r_ref[0, rows, :] * e_in
        b_g = bb * e_rem
        k_g = k2 * e_rem
        vv = v_ref[0, rows, :]
        gam = jnp.exp(tot)
        for grp in range(ngroups):
            ls = slice(grp * n, (grp + 1) * n)
            a_s, b_s, k_s, r_s = (_stack_heads(t[:, ls]) for t in (a_t, b_t, k_t, r_t))
            v_s, bg_s, kg_s = (_stack_heads(t[:, ls]) for t in (vv, b_g, k_g))
            gram = _mm(jnp.concatenate([a_s, r_s], axis=0), jnp.concatenate([b_s, k_s], axis=0), _NT)
            nmat = jnp.where(strict, gram[:n, :n], 0.0)
            m_ak = jnp.where(strict, gram[:n, n:], 0.0)
            m_rb = jnp.where(incl, gram[n:, :n], 0.0)
            m_rk = jnp.where(incl, gram[n:, n:], 0.0)
            winv = jnp.where(eye, 1.0, 0.0) + nmat
            pw = nmat
            for _ in range(int(math.log2(cl_)) - 1):
                pw = _mm(pw, pw)
                winv = winv + _mm(winv, pw)
            st = st_ref[grp]
            ar_st = _mm(jnp.concatenate([a_s, r_s], axis=0), st)
            u_s = _mm(winv, ar_st[:n] + _mm(m_ak, v_s))
            y_s = ar_st[n:] + _mm(m_rb, u_s) + _mm(m_rk, v_s)
            y4 = y_s[0:cl_]
            for hh in range(1, n // cl_):
                y4 = y4 + y_s[hh * cl_:(hh + 1) * cl_]
            y_ref[rows, ls] = y4
            colg = jnp.sum(jnp.where(eye, jnp.broadcast_to(gam[:, ls], (n, n)), 0.0), axis=1, keepdims=True)
            st_ref[grp] = colg * st + _mm(jnp.concatenate([bg_s, kg_s], axis=0),
                                          jnp.concatenate([u_s, v_s], axis=0), _TN)
        return 0

    lax.fori_loop(0, ts // cl_, chunk, 0)

    y = y_ref[...]
    mean = _mm_exact_rhs(y, havg_ref[...])
    yc = y - mean
    var = _mm_exact_rhs(yc * yc, havg_ref[...])
    y = yc * lax.rsqrt(var + RWKV_GN_EPS) * lnw_ref[...] + lnb_ref[...]
    o_ref[0] = (y + bonus_ref[0]) * g_ref[0]


def rwkv7_time_mix(proj, mu, w0, w_up, a0, a_up, g_up, k_k, k_a, r_k, ln_w, ln_b, *, ts):
    bsz, seq, _ = proj.shape
    w = RWKV_WIDTH
    pw = 3 * w + 2 * LANES
    assert RWKV_W_LORA + RWKV_A_LORA == LANES and RWKV_G_LORA == LANES and seq % ts == 0
    wup = jnp.concatenate([w_up, jnp.zeros((RWKV_A_LORA, w), F32)], axis=0)
    aup = jnp.concatenate([jnp.zeros((RWKV_W_LORA, w), F32), a_up], axis=0)
    hsum = _head_sum_matrix(w)
    row = lambda t: t.reshape(1, -1)
    full = lambda shape: pl.BlockSpec(shape, lambda b, s: (0,) * len(shape))
    tile = pl.BlockSpec((1, ts, w), lambda b, s: (b, s, 0))
    nprev = ts // SUBLANES
    outs = pl.pallas_call(
        _rwkv_prep_kernel,
        grid=(bsz, seq // ts),
        in_specs=[
            pl.BlockSpec((1, ts, pw), lambda b, s: (b, s, 0)),
            pl.BlockSpec((1, SUBLANES, pw), lambda b, s: (b, jnp.maximum(s * nprev - 1, 0), 0)),
            full((1, pw)), full((1, w)), full((LANES, w)), full((1, w)), full((LANES, w)),
            full((LANES, w)), full((1, w)), full((1, w)), full((1, w)), full((w, w)),
        ],
        out_specs=[tile] * 8,
        out_shape=[jax.ShapeDtypeStruct((bsz, seq, w), F32)] * 8,
        compiler_params=_cparams(("parallel", "parallel")),
        name="rwkv_prep",
    )(proj, proj, row(mu), row(w0), wup, row(a0), aup, g_up.astype(BF16), row(k_k), row(k_a),
      row(r_k), hsum)
    havg = (_head_sum_matrix(w).astype(F32) / HEAD_DIM).astype(BF16)
    return pl.pallas_call(
        _rwkv_scan_kernel,
        grid=(bsz, seq // ts),
        in_specs=[tile] * 8 + [full((1, w)), full((1, w)), full((w, w))],
        out_specs=tile,
        out_shape=jax.ShapeDtypeStruct((bsz, seq, w), F32),
        scratch_shapes=[pltpu.VMEM((ts, w), F32),
                        pltpu.VMEM((w // RWKV_GROUP, RWKV_GROUP, RWKV_GROUP), F32)],
        compiler_params=_cparams(("parallel", "arbitrary")),
        name="rwkv_scan",
    )(*outs, row(ln_w), row(ln_b), havg)


DSA_KV_TILE = 256
DSA_SEARCH_TILE = 512
DSA_COUNT_ROWS = 64
DSA_VROWS = 128
INT32_MIN = -2 ** 31
NEG_INF_KEY = -2139095041


def _sortable_key(x):
    bits = pltpu.bitcast(x + 0.0, jnp.int32)
    return jnp.where(bits < 0, bits ^ jnp.int32(0x7FFFFFFF), bits)


def _dsa_kernel(q_ref, qi_ref, wi_ref, k_ref, v_ref, ki_ref, o_ref,
                kb_ref, vt_ref, kib_ref, key_ref, acc_ref):
    i = pl.program_id(1)
    nq, kt, st = DSA_QBLOCK, DSA_KV_TILE, DSA_SEARCH_TILE
    seq = k_ref.shape[1]
    topk = min(DSA_TOPK, seq // 4)

    @pl.when(i == 0)
    def _():
        kb_ref[...] = k_ref[0].astype(BF16)
        vrow = lax.broadcasted_iota(jnp.int32, (DSA_VROWS, seq), 0)
        ones_row = jnp.where(vrow == HEAD_DIM, 1.0, 0.0)
        vt_ref[...] = jnp.where(vrow < HEAD_DIM, v_ref[0].T[:DSA_VROWS], ones_row).astype(BF16)
        kib_ref[...] = ki_ref[0].astype(BF16)

    n_search = (i * nq) // st + 1
    spos = i * nq + lax.broadcasted_iota(jnp.int32, (st, nq), 1)
    lane = lax.broadcasted_iota(jnp.int32, (nq, LANES), 1)
    qpos = i * nq + lax.broadcasted_iota(jnp.int32, (kt, nq), 1)
    krow = lax.broadcasted_iota(jnp.int32, (kt, nq), 0)
    srow = lax.broadcasted_iota(jnp.int32, (st, nq), 0)

    def stack_heads(ref, nheads):
        parts = []
        for h in range(nheads):
            pair = ref[0, :, (h // 2) * LANES:(h // 2 + 1) * LANES]
            parts.append(jnp.where(lane // HEAD_DIM == h % 2, pair, 0.0))
        return jnp.concatenate(parts, axis=0)

    qib = stack_heads(qi_ref, DSA_IDX_HEADS).astype(BF16)
    wi_t = (wi_ref[0] * (DSA_IDX_HEADS ** -0.5 * HEAD_DIM ** -0.5)).T
    w_lanes = jnp.concatenate([wi_t[h:h + 1, :] for h in range(DSA_IDX_HEADS)], axis=1)

    def score_tile(c, _):
        rows = pl.ds(pl.multiple_of(c * kt, kt), kt)
        logit = lax.dot_general(kib_ref[rows, :], qib, _NT, preferred_element_type=F32)
        weighted = jnp.maximum(logit, 0.0) * w_lanes
        score = weighted[:, 0:nq]
        for h in range(1, DSA_IDX_HEADS):
            score = score + weighted[:, h * nq:(h + 1) * nq]
        score = jnp.where(c * kt + krow <= qpos, score, -jnp.inf)
        key_ref[rows, :] = _sortable_key(score)
        return 0

    lax.fori_loop(0, n_search * (st // kt), score_tile, 0)

    def count(pred):
        def tile(c, acc):
            rows = pl.ds(pl.multiple_of(c * st, st), st)
            hit = jnp.where(pred(key_ref[rows, :], c * st + srow), 1.0, 0.0)
            for part in range(st // DSA_COUNT_ROWS):
                acc = acc + hit[part * DSA_COUNT_ROWS:(part + 1) * DSA_COUNT_ROWS]
            return acc
        acc = lax.fori_loop(0, n_search, tile, jnp.zeros((DSA_COUNT_ROWS, nq), F32))
        return jnp.sum(acc, axis=0, keepdims=True)

    def search_step(t, base):
        cand = base + lax.shift_left(jnp.int32(1), 31 - t)
        cnt = count(lambda key, pos: key >= cand)
        return jnp.where(cnt >= topk, cand, base)

    thr = lax.fori_loop(0, 32, search_step, jnp.full((1, nq), INT32_MIN, jnp.int32))
    need = topk - count(lambda key, pos: key > thr)
    n_eq = count(lambda key, pos: key == thr)
    tied = jnp.max(jnp.where((n_eq > need) & (thr > NEG_INF_KEY), 1.0, 0.0)) > 0.0
    nbits = seq.bit_length()

    def tie_search(_):
        def step(t, pos):
            cand = pos + lax.shift_left(jnp.int32(1), nbits - 1 - t)
            cnt = count(lambda key, kpos: (key == thr) & (kpos < cand))
            return jnp.where(cnt < need, cand, pos)
        return lax.fori_loop(0, nbits, step, jnp.zeros((1, nq), jnp.int32))

    last = lax.cond(tied, tie_search, lambda _: jnp.full((1, nq), seq, jnp.int32), 0)

    qs = (stack_heads(q_ref, DSA_HEADS) * (HEAD_DIM ** -0.5 * LOG2_E)).astype(BF16)
    acc_ref[...] = jnp.zeros_like(acc_ref)

    def attn_tile(c, m_all):
        rows = pl.ds(pl.multiple_of(c * st, st), st)
        key = key_ref[rows, :]
        kpos = c * st + srow
        chosen = ((key > thr) | ((key == thr) & (kpos <= last))) & (kpos <= spos)
        bias = jnp.where(chosen, 0.0, NEG_BIG)
        kb, vt = kb_ref[rows, :], vt_ref[:, rows]
        m_rows = []
        for h in range(DSA_HEADS):
            s = lax.dot_general(kb, qs[h * nq:(h + 1) * nq], _NT, preferred_element_type=F32) + bias
            m_prev = m_all[h:h + 1, :]
            m_new = jnp.maximum(m_prev, jnp.max(s, axis=0, keepdims=True))
            p = jnp.exp2(s - m_new).astype(BF16)
            acc_ref[h] = jnp.exp2(m_prev - m_new) * acc_ref[h] + jnp.dot(vt, p, preferred_element_type=F32)
            m_rows.append(m_new)
        return jnp.concatenate(m_rows, axis=0)

    lax.fori_loop(0, n_search, attn_tile, jnp.full((DSA_HEADS, nq), NEG_BIG, F32))
    for pr in range(DSA_WIDTH // LANES):
        halves = []
        for h in (2 * pr, 2 * pr + 1):
            acc = acc_ref[h]
            halves.append(acc[:HEAD_DIM] / acc[HEAD_DIM:HEAD_DIM + 1])
        o_ref[0, :, pr * LANES:(pr + 1) * LANES] = jnp.concatenate(halves, axis=0).T


def dsa_attention(proj, *, q_col, qi_col, wi_col, k_col, v_col, ki_col):
    bsz, seq, _ = proj.shape
    assert seq % DSA_SEARCH_TILE == 0 and DSA_SEARCH_TILE % DSA_KV_TILE == 0 and DSA_KV_TILE % DSA_QBLOCK == 0
    nq = seq // DSA_QBLOCK
    kvspec = lambda col: pl.BlockSpec((1, seq, LANES), lambda b, i: (b, 0, col // LANES))
    return pl.pallas_call(
        _dsa_kernel,
        grid=(bsz, nq),
        in_specs=[
            pl.BlockSpec((1, DSA_QBLOCK, DSA_WIDTH), lambda b, i: (b, i, q_col // DSA_WIDTH)),
            pl.BlockSpec((1, DSA_QBLOCK, 2 * LANES), lambda b, i: (b, i, qi_col // (2 * LANES))),
            pl.BlockSpec((1, DSA_QBLOCK, LANES), lambda b, i: (b, i, wi_col // LANES)),
            kvspec(k_col), kvspec(v_col), kvspec(ki_col),
        ],
        out_specs=pl.BlockSpec((1, DSA_QBLOCK, DSA_WIDTH), lambda b, i: (b, i, 0)),
        out_shape=jax.ShapeDtypeStruct((bsz, seq, DSA_WIDTH), F32),
        scratch_shapes=[pltpu.VMEM((seq, LANES), BF16), pltpu.VMEM((DSA_VROWS, seq), BF16),
                        pltpu.VMEM((seq, LANES), BF16),
                        pltpu.VMEM((seq, DSA_QBLOCK), jnp.int32),
                        pltpu.VMEM((DSA_HEADS, DSA_VROWS, DSA_QBLOCK), F32)],
        compiler_params=_cparams(("parallel", "arbitrary")),
        name="dsa_attention",
    )(proj, proj, proj, proj, proj, proj)


MOE_TOKEN_TILE = 2048
MOE_ROW_BLOCK = 256


def _router_kernel(h_ref, g_ref, wr_ref, xn_ref, route_ref):
    xn = _rms(h_ref[...], g_ref[...])
    xn_ref[...] = xn.astype(BF16)
    lane = lax.broadcasted_iota(jnp.int32, (xn.shape[0], LANES), 1)
    logits = jnp.where(lane < N_EXPERTS, _mm3(xn, wr_ref[...]), -jnp.inf)
    v1 = jnp.max(logits, axis=1, keepdims=True)
    e1 = jnp.min(jnp.where(logits == v1, lane, LANES), axis=1, keepdims=True)
    rest = jnp.where(lane == e1, -jnp.inf, logits)
    v2 = jnp.max(rest, axis=1, keepdims=True)
    e2 = jnp.min(jnp.where(rest == v2, lane, LANES), axis=1, keepdims=True)
    ratio = jnp.exp(v2 - v1)
    g1 = 1.0 / (1.0 + ratio)
    g2 = ratio * g1
    route_ref[...] = jnp.where(lane == 0, e1.astype(F32), jnp.where(
        lane == 1, e2.astype(F32), jnp.where(lane == 2, g1, jnp.where(lane == 3, g2, 0.0))))


def route_tokens(h2d, g, router, *, tm):
    t, d = h2d.shape
    wr = jnp.zeros((d, LANES), F32).at[:, :N_EXPERTS].set(router)
    return pl.pallas_call(
        _router_kernel,
        grid=(t // tm,),
        in_specs=[pl.BlockSpec((tm, d), lambda i: (i, 0)), pl.BlockSpec((1, d), lambda i: (0, 0)),
                  pl.BlockSpec((d, LANES), lambda i: (0, 0))],
        out_specs=[pl.BlockSpec((tm, d), lambda i: (i, 0)), pl.BlockSpec((tm, LANES), lambda i: (i, 0))],
        out_shape=[jax.ShapeDtypeStruct((t, d), BF16), jax.ShapeDtypeStruct((t, LANES), F32)],
        compiler_params=_cparams(("parallel",)),
        name="moe_router",
    )(h2d, g.reshape(1, d), wr)


def _moe_kernel(cnt_ref, xn_ref, rk_ref, gt_ref, rkt_ref, wg_ref, wu_ref, wd_ref, o_ref, xs_ref, acc_ref):
    tau, e, f = pl.program_id(0), pl.program_id(1), pl.program_id(2)
    tt = xn_ref.shape[0]
    rb = MOE_ROW_BLOCK
    nblk = (cnt_ref[tau * N_EXPERTS + e] + rb - 1) // rb
    slot = lax.broadcasted_iota(jnp.int32, (rb, tt), 0)

    @pl.when((e == 0) & (f == 0))
    def _():
        o_ref[...] = jnp.zeros_like(o_ref)

    def rows_of(b):
        return pl.ds(pl.multiple_of(b * rb, rb), rb)

    @pl.when(f == 0)
    def _():
        rank = rk_ref[0, pl.ds(e, 1), :]

        def gather(b, _):
            onehot = jnp.where(rank == slot + b * rb, 1.0, 0.0).astype(BF16)
            xs_ref[rows_of(b), :] = jnp.dot(onehot, xn_ref[...], preferred_element_type=F32).astype(BF16)
            acc_ref[rows_of(b), :] = jnp.zeros((rb, acc_ref.shape[1]), F32)
            return 0

        lax.fori_loop(0, nblk, gather, 0)

    def ffn(b, _):
        xs = xs_ref[rows_of(b), :]
        gate = jnp.dot(xs, wg_ref[0], preferred_element_type=F32)
        up = jnp.dot(xs, wu_ref[0], preferred_element_type=F32)
        act = (gate * jax.nn.sigmoid(gate) * up).astype(BF16)
        acc_ref[rows_of(b), :] += jnp.dot(act, wd_ref[0], preferred_element_type=F32)
        return 0

    lax.fori_loop(0, nblk, ffn, 0)

    @pl.when(f == pl.num_programs(2) - 1)
    def _():
        rank = rk_ref[0, pl.ds(e, 1), :]
        gates = gt_ref[0, pl.ds(e, 1), :]
        lane_e = lax.broadcasted_iota(jnp.int32, rkt_ref.shape[1:], 1)
        rank_t = jnp.sum(jnp.where(lane_e == e, rkt_ref[0], 0), axis=1, keepdims=True)
        slot_t = lax.broadcasted_iota(jnp.int32, (tt, rb), 1)

        def combine(b, _):
            hit = rank == slot + b * rb
            gate_slot = jnp.sum(jnp.where(hit, gates, 0.0), axis=1, keepdims=True)
            yg = (acc_ref[rows_of(b), :] * gate_slot).astype(BF16)
            scatter = jnp.where(rank_t == slot_t + b * rb, 1.0, 0.0).astype(BF16)
            o_ref[...] += jnp.dot(scatter, yg, preferred_element_type=F32)
            return 0

        lax.fori_loop(0, nblk, combine, 0)


def moe_experts(xn_bf16, route, w_gate, w_up, w_down, *, tf):
    t, d = xn_bf16.shape
    tt = MOE_TOKEN_TILE
    ntile = t // tt
    fdim = w_gate.shape[2]
    experts = route[:, 0:2].astype(jnp.int32).reshape(ntile, tt * 2)
    gates = route[:, 2:4].reshape(ntile, tt * 2)
    onehot = (experts[:, :, None] == jnp.arange(N_EXPERTS)[None, None, :]).astype(jnp.int32)
    csum = jnp.cumsum(onehot, axis=1)
    counts = csum[:, -1, :]
    rank = jnp.where(onehot > 0, csum - 1, -1)
    rank_tok = jnp.max(rank.reshape(ntile, tt, 2, N_EXPERTS), axis=2)
    gate_tok = jnp.sum((onehot * gates[:, :, None]).reshape(ntile, tt, 2, N_EXPERTS), axis=2)
    rk = jnp.swapaxes(rank_tok, 1, 2)
    gt = jnp.swapaxes(gate_tok, 1, 2)
    grid_spec = pltpu.PrefetchScalarGridSpec(
        num_scalar_prefetch=1,
        grid=(ntile, N_EXPERTS, fdim // tf),
        in_specs=[
            pl.BlockSpec((tt, d), lambda i, e, f, c: (i, 0)),
            pl.BlockSpec((1, N_EXPERTS, tt), lambda i, e, f, c: (i, 0, 0)),
            pl.BlockSpec((1, N_EXPERTS, tt), lambda i, e, f, c: (i, 0, 0)),
            pl.BlockSpec((1, tt, N_EXPERTS), lambda i, e, f, c: (i, 0, 0)),
            pl.BlockSpec((1, d, tf), lambda i, e, f, c: (e, 0, f)),
            pl.BlockSpec((1, d, tf), lambda i, e, f, c: (e, 0, f)),
            pl.BlockSpec((1, tf, d), lambda i, e, f, c: (e, f, 0)),
        ],
        out_specs=pl.BlockSpec((tt, d), lambda i, e, f, c: (i, 0)),
        scratch_shapes=[pltpu.VMEM((tt, d), BF16), pltpu.VMEM((tt, d), F32)],
    )
    return pl.pallas_call(
        _moe_kernel,
        grid_spec=grid_spec,
        out_shape=jax.ShapeDtypeStruct((t, d), F32),
        compiler_params=_cparams(("parallel", "arbitrary", "arbitrary")),
        name="moe_experts",
    )(counts.reshape(-1), xn_bf16, rk, gt, rank_tok, w_gate, w_up, w_down)


def _add_norm_kernel(a_ref, b_ref, g_ref, o_ref, *, normalize):
    y = a_ref[...] + b_ref[...]
    o_ref[...] = _rms(y, g_ref[...]) if normalize else y


def add_rmsnorm(a, b, g, *, tm):
    t, d = a.shape
    spec = pl.BlockSpec((tm, d), lambda i: (i, 0))
    gain = jnp.ones((1, d), F32) if g is None else g.reshape(1, d)
    return pl.pallas_call(
        functools.partial(_add_norm_kernel, normalize=g is not None),
        grid=(t // tm,),
        in_specs=[spec, spec, pl.BlockSpec((1, d), lambda i: (0, 0))],
        out_specs=spec,
        out_shape=jax.ShapeDtypeStruct((t, d), F32),
        compiler_params=_cparams(("parallel",)),
        name="add_rmsnorm",
    )(a, b, gain)


ROW_TILE = 512
FFN_ROW_TILE = 1024
FFN_COL_TILE = 256

EVEN_Q_COL, EVEN_K_COL, EVEN_V_COL = 512, 1024, 1536
EVEN_ROPE_COLS = np.zeros(2048, bool)
EVEN_ROPE_COLS[EVEN_Q_COL:EVEN_V_COL] = True


def _odd_layout():
    rw = 3 * RWKV_WIDTH + RWKV_W_LORA + RWKV_A_LORA + RWKV_G_LORA
    q0 = rw
    k0 = q0 + DSA_WIDTH
    v0 = k0 + HEAD_DIM
    qi0 = v0 + HEAD_DIM
    ki0 = qi0 + DSA_IDX_HEADS * HEAD_DIM
    wi0 = ki0 + HEAD_DIM
    seg = lambda start, n: list(range(start, start + n))
    src, keep, rope = [], [], []

    def add(cols, roped, pad=0):
        src.extend(cols + [0] * pad)
        keep.extend([1.0] * len(cols) + [0.0] * pad)
        rope.extend([roped] * (len(cols) + pad))

    add(seg(0, rw), False)
    cols = {"k_col": len(src)}
    add(seg(k0, HEAD_DIM) * 2, True)
    cols["v_col"] = len(src)
    add(seg(v0, HEAD_DIM) * 2, False)
    cols["q_col"] = len(src)
    add(seg(q0, DSA_WIDTH), True)
    cols["qi_col"] = len(src)
    add(seg(qi0, DSA_IDX_HEADS * HEAD_DIM), True)
    cols["ki_col"] = len(src)
    add(seg(ki0, HEAD_DIM) * 2, True)
    cols["wi_col"] = len(src)
    add(seg(wi0, DSA_IDX_HEADS), False, pad=LANES - DSA_IDX_HEADS)
    return (np.asarray(src, np.int32), np.asarray(keep, np.float32), np.asarray(rope, bool), cols)


ODD_COL_SRC, ODD_COL_KEEP, ODD_ROPE_COLS, ODD_DSA_COLS = _odd_layout()


def kernel(x, e_norm_mix, e_w_in, s5_a_re, s5_a_im, s5_log_dt, s5_b_re, s5_b_im, s5_c_re, s5_c_im, s5_d, s5_glu_w, s5_glu_b, e_w_out, e_norm_ffn, ffn_w_gate, ffn_w_up, ffn_w_down, o_norm_mix, o_w_in, rwkv_mu, rwkv_w0, rwkv_w_up, rwkv_a0, rwkv_a_up, rwkv_g_up, rwkv_k_k, rwkv_k_a, rwkv_r_k, rwkv_ln_w, rwkv_ln_b, o_w_out, o_norm_ffn, moe_router, moe_w_gate, moe_w_up, moe_w_down, final_norm):
    bsz, seq, d = x.shape
    n_even, n_odd = e_norm_mix.shape[0], o_norm_mix.shape[0]
    h = x.reshape(-1, d)
    zero = jnp.zeros((bsz * seq, d), F32)
    for layer in range(n_even + n_odd):
        j = layer // 2
        last = layer == n_even + n_odd - 1
        if layer % 2 == 0:
            proj = norm_matmul_rope(h, e_norm_mix[j], e_w_in[j].astype(BF16), EVEN_ROPE_COLS, seq,
                                    tm=ROW_TILE, tn=512).reshape(bsz, seq, -1)
            ya = s5_mixer(proj, s5_a_re[j], s5_a_im[j], s5_log_dt[j], s5_b_re[j], s5_b_im[j], s5_c_re[j],
                          s5_c_im[j], s5_d[j], s5_glu_w[j], s5_glu_b[j], ts=256)
            yb = moba_attention(proj, q_col=EVEN_Q_COL, k_col=EVEN_K_COL, v_col=EVEN_V_COL)
            h = out_proj_residual(h, ya.reshape(-1, S5_WIDTH), yb.reshape(-1, MOBA_WIDTH),
                                  e_w_out[j].astype(BF16), tm=ROW_TILE)
            h = dense_ffn_residual(h, e_norm_ffn[j], ffn_w_gate[j].astype(BF16), ffn_w_up[j].astype(BF16),
                                   ffn_w_down[j].astype(BF16), tm=FFN_ROW_TILE, tf=FFN_COL_TILE)
            if last:
                h = add_rmsnorm(h, zero, final_norm, tm=ROW_TILE)
        else:
            w_in = jnp.take(o_w_in[j], jnp.asarray(ODD_COL_SRC), axis=1) * jnp.asarray(ODD_COL_KEEP)[None, :]
            proj = norm_matmul_rope(h, o_norm_mix[j], w_in.astype(BF16), ODD_ROPE_COLS, seq,
                                    tm=ROW_TILE, tn=512).reshape(bsz, seq, -1)
            yc = rwkv7_time_mix(proj, rwkv_mu[j], rwkv_w0[j], rwkv_w_up[j], rwkv_a0[j], rwkv_a_up[j],
                                rwkv_g_up[j], rwkv_k_k[j], rwkv_k_a[j], rwkv_r_k[j], rwkv_ln_w[j],
                                rwkv_ln_b[j], ts=256)
            yd = dsa_attention(proj, **ODD_DSA_COLS)
            h = out_proj_residual(h, yc.reshape(-1, RWKV_WIDTH), yd.reshape(-1, DSA_WIDTH),
                                  o_w_out[j].astype(BF16), tm=ROW_TILE)
            xn, route = route_tokens(h, o_norm_ffn[j], moe_router[j], tm=ROW_TILE)
            y = moe_experts(xn, route, moe_w_gate[j].astype(BF16), moe_w_up[j].astype(BF16),
                            moe_w_down[j].astype(BF16), tf=512)
            h = add_rmsnorm(h, y, final_norm if last else None, tm=ROW_TILE)
    return h.reshape(bsz, seq, d)
```

```python
import functools
import math

import jax
import jax.numpy as jnp
import numpy as np
from jax import lax
from jax.experimental import pallas as pl
from jax.experimental.pallas import tpu as pltpu

F32 = jnp.float32
BF16 = jnp.bfloat16
HIGHEST = lax.Precision.HIGHEST

LANES = 128
SUBLANES = 8
VMEM_LIMIT_BYTES = 56 * 1024 * 1024

D_MODEL = 1024
HEAD_DIM = 64
ROT_DIM = HEAD_DIM // 4
ROPE_THETA = 500000.0
NORM_EPS = 1e-6
S5_WIDTH = 512
S5_GROUP = 16
S5_GROUPS = S5_WIDTH // S5_GROUP
S5_STATE = 64
S5_NSTATE = S5_GROUPS * S5_STATE
MOBA_WIDTH = 512
MOBA_HEADS = MOBA_WIDTH // HEAD_DIM
MOBA_BLOCK = 256
MOBA_TOPK = 3
MOBA_BLOCKS_PER_STEP = 4
MOBA_STREAMS = 1
RWKV_WIDTH = 512
RWKV_HEADS = RWKV_WIDTH // HEAD_DIM
RWKV_W_LORA = 64
RWKV_A_LORA = 64
RWKV_G_LORA = 128
RWKV_GN_EPS = 64e-5
RWKV_CHUNK = 64
DSA_WIDTH = 512
DSA_HEADS = DSA_WIDTH // HEAD_DIM
DSA_IDX_HEADS = 4
DSA_TOPK = 256
DSA_QBLOCK = 128
N_EXPERTS = 8
NEG_BIG = -1e30
LOG2_E = 1.4426950408889634


def _cparams(sem):
    return pltpu.CompilerParams(dimension_semantics=sem, vmem_limit_bytes=VMEM_LIMIT_BYTES)


def _rms(x, g):
    return x * lax.rsqrt(jnp.mean(x * x, axis=-1, keepdims=True) + NORM_EPS) * g


def _norm_matmul_kernel(flags_ref, h_ref, g_ref, w_ref, cos_ref, sina_ref, sinb_ref, colf_ref,
                        o_ref, xn_ref):
    j = pl.program_id(1)

    @pl.when(j == 0)
    def _():
        xn_ref[...] = _rms(h_ref[...], g_ref[...]).astype(BF16)

    acc = jnp.dot(xn_ref[...], w_ref[...], preferred_element_type=F32)
    tn = acc.shape[1]

    @pl.when(flags_ref[j] == 0)
    def _():
        o_ref[...] = acc

    @pl.when(flags_ref[j] != 0)
    def _():
        reps = tn // LANES
        colf = colf_ref[...]
        cos = 1.0 + colf * (jnp.tile(cos_ref[...], (1, reps)) - 1.0)
        sina = colf * jnp.tile(sina_ref[...], (1, reps))
        sinb = colf * jnp.tile(sinb_ref[...], (1, reps))
        nxt = pltpu.roll(acc, tn - ROT_DIM // 2, axis=1)
        prv = pltpu.roll(acc, ROT_DIM // 2, axis=1)
        o_ref[...] = acc * cos + nxt * sina + prv * sinb


def _rope_tables(seq):
    half = ROT_DIM // 2
    inv_freq = ROPE_THETA ** (-jnp.arange(half, dtype=F32) / half)
    ang = jnp.arange(seq, dtype=F32)[:, None] * inv_freq[None, :]
    cos, sin = jnp.cos(ang), jnp.sin(ang)
    ones = jnp.ones((seq, HEAD_DIM - ROT_DIM), F32)
    zeros = jnp.zeros((seq, HEAD_DIM - ROT_DIM), F32)
    zh = jnp.zeros((seq, half), F32)
    cos64 = jnp.concatenate([cos, cos, ones], axis=1)
    sina64 = jnp.concatenate([-sin, zh, zeros], axis=1)
    sinb64 = jnp.concatenate([zh, sin, zeros], axis=1)
    rep = LANES // HEAD_DIM
    return jnp.tile(cos64, (1, rep)), jnp.tile(sina64, (1, rep)), jnp.tile(sinb64, (1, rep))


def norm_matmul_rope(h2d, g, w_bf16, rope_cols, seq, *, tm, tn):
    t, d = h2d.shape
    n = w_bf16.shape[1]
    assert t % tm == 0 and n % tn == 0 and seq % tm == 0 and tn % LANES == 0
    nseq = seq // tm
    cos, sina, sinb = _rope_tables(seq)
    tile_flags = jnp.asarray(rope_cols.reshape(n // tn, tn).any(axis=1).astype(np.int32))
    colf = jnp.asarray(rope_cols.astype(np.float32))[None, :]
    grid_spec = pltpu.PrefetchScalarGridSpec(
        num_scalar_prefetch=1,
        grid=(t // tm, n // tn),
        in_specs=[
            pl.BlockSpec((tm, d), lambda i, j, f: (i, 0)),
            pl.BlockSpec((1, d), lambda i, j, f: (0, 0)),
            pl.BlockSpec((d, tn), lambda i, j, f: (0, j)),
            pl.BlockSpec((tm, LANES), lambda i, j, f: (i % nseq, 0)),
            pl.BlockSpec((tm, LANES), lambda i, j, f: (i % nseq, 0)),
            pl.BlockSpec((tm, LANES), lambda i, j, f: (i % nseq, 0)),
            pl.BlockSpec((1, tn), lambda i, j, f: (0, j)),
        ],
        out_specs=pl.BlockSpec((tm, tn), lambda i, j, f: (i, j)),
        scratch_shapes=[pltpu.VMEM((tm, d), BF16)],
    )
    return pl.pallas_call(
        _norm_matmul_kernel,
        grid_spec=grid_spec,
        out_shape=jax.ShapeDtypeStruct((t, n), F32),
        compiler_params=_cparams(("parallel", "arbitrary")),
        name="norm_matmul_rope",
    )(tile_flags, h2d, g.reshape(1, d), w_bf16, cos, sina, sinb, colf)


def _out_proj_kernel(res_ref, a_ref, b_ref, wa_ref, wb_ref, o_ref):
    acc = jnp.dot(a_ref[...].astype(BF16), wa_ref[...], preferred_element_type=F32)
    acc += jnp.dot(b_ref[...].astype(BF16), wb_ref[...], preferred_element_type=F32)
    o_ref[...] = res_ref[...] + acc


def out_proj_residual(res, a, b, w_bf16, *, tm):
    t, d = res.shape
    ka, kb = a.shape[1], b.shape[1]
    wa, wb = w_bf16[:ka], w_bf16[ka:]
    return pl.pallas_call(
        _out_proj_kernel,
        grid=(t // tm,),
        in_specs=[
            pl.BlockSpec((tm, d), lambda i: (i, 0)),
            pl.BlockSpec((tm, ka), lambda i: (i, 0)),
            pl.BlockSpec((tm, kb), lambda i: (i, 0)),
            pl.BlockSpec((ka, d), lambda i: (0, 0)),
            pl.BlockSpec((kb, d), lambda i: (0, 0)),
        ],
        out_specs=pl.BlockSpec((tm, d), lambda i: (i, 0)),
        out_shape=jax.ShapeDtypeStruct((t, d), F32),
        compiler_params=_cparams(("parallel",)),
        name="out_proj_residual",
    )(res, a, b, wa, wb)


def _dense_ffn_kernel(h_ref, g_ref, wg_ref, wu_ref, wd_ref, o_ref, xn_ref, acc_ref):
    f = pl.program_id(1)

    @pl.when(f == 0)
    def _():
        xn_ref[...] = _rms(h_ref[...], g_ref[...]).astype(BF16)
        acc_ref[...] = jnp.zeros_like(acc_ref)

    xn = xn_ref[...]
    gate = jnp.dot(xn, wg_ref[...], preferred_element_type=F32)
    up = jnp.dot(xn, wu_ref[...], preferred_element_type=F32)
    act = (gate * jax.nn.sigmoid(gate) * up).astype(BF16)
    acc_ref[...] += jnp.dot(act, wd_ref[...], preferred_element_type=F32)

    @pl.when(f == pl.num_programs(1) - 1)
    def _():
        o_ref[...] = h_ref[...] + acc_ref[...]


def dense_ffn_residual(h2d, g, wg, wu, wd, *, tm, tf):
    t, d = h2d.shape
    fdim = wg.shape[1]
    assert t % tm == 0 and fdim % tf == 0
    return pl.pallas_call(
        _dense_ffn_kernel,
        grid=(t // tm, fdim // tf),
        in_specs=[
            pl.BlockSpec((tm, d), lambda i, f: (i, 0)),
            pl.BlockSpec((1, d), lambda i, f: (0, 0)),
            pl.BlockSpec((d, tf), lambda i, f: (0, f)),
            pl.BlockSpec((d, tf), lambda i, f: (0, f)),
            pl.BlockSpec((tf, d), lambda i, f: (f, 0)),
        ],
        out_specs=pl.BlockSpec((tm, d), lambda i, f: (i, 0)),
        out_shape=jax.ShapeDtypeStruct((t, d), F32),
        scratch_shapes=[pltpu.VMEM((tm, d), BF16), pltpu.VMEM((tm, d), F32)],
        compiler_params=_cparams(("parallel", "arbitrary")),
        name="dense_ffn_residual",
    )(h2d, g.reshape(1, d), wg, wu, wd)


S5_LANE_CHUNK = 512
S5_MXU_TILE = 256


def _s5_kernel(u_ref, bre_ref, bim_ref, cre_ref, cim_ref, stepr_ref, stepi_ref, carr_ref, cari_ref,
               d_ref, gw_ref, gb_ref, o_ref, xr_ref, xi_ref, sr_ref, si_ref):
    s = pl.program_id(1)
    ts = u_ref.shape[1]

    @pl.when(s == 0)
    def _():
        sr_ref[...] = jnp.zeros_like(sr_ref)
        si_ref[...] = jnp.zeros_like(si_ref)

    u = u_ref[0]
    ub = u.astype(BF16)
    ratio = S5_NSTATE // S5_WIDTH
    for n in range(S5_NSTATE // S5_MXU_TILE):
        cols = slice(n * S5_MXU_TILE, (n + 1) * S5_MXU_TILE)
        chans = slice((n // ratio) * S5_MXU_TILE, (n // ratio + 1) * S5_MXU_TILE)
        xr_ref[:, cols] = jnp.dot(ub[:, chans], bre_ref[chans, cols], preferred_element_type=F32)
        xi_ref[:, cols] = jnp.dot(ub[:, chans], bim_ref[chans, cols], preferred_element_type=F32)

    for c in range(S5_NSTATE // S5_LANE_CHUNK):
        lanes = pl.ds(c * S5_LANE_CHUNK, S5_LANE_CHUNK)
        steps = [(stepr_ref[k, :, lanes], stepi_ref[k, :, lanes]) for k in range(3)]
        pr, pi = carr_ref[:, lanes], cari_ref[:, lanes]

        def body(g, carry, lanes=lanes, steps=steps, pr=pr, pi=pi):
            cr, ci = carry
            rows = pl.ds(pl.multiple_of(g * SUBLANES, SUBLANES), SUBLANES)
            xr = xr_ref[rows, lanes]
            xi = xi_ref[rows, lanes]
            for k, (ar, ai) in enumerate(steps):
                rr = pltpu.roll(xr, 1 << k, axis=0)
                ri = pltpu.roll(xi, 1 << k, axis=0)
                xr, xi = xr + ar * rr - ai * ri, xi + ar * ri + ai * rr
            xr = xr + pr * cr - pi * ci
            xi = xi + pr * ci + pi * cr
            xr_ref[rows, lanes] = xr
            xi_ref[rows, lanes] = xi
            return xr[SUBLANES - 1:SUBLANES, :], xi[SUBLANES - 1:SUBLANES, :]

        cr, ci = lax.fori_loop(0, ts // SUBLANES, body, (sr_ref[:, lanes], si_ref[:, lanes]))
        sr_ref[:, lanes] = cr
        si_ref[:, lanes] = ci

    y_tiles = []
    for m in range(S5_WIDTH // S5_MXU_TILE):
        chans = slice(m * S5_MXU_TILE, (m + 1) * S5_MXU_TILE)
        states = slice(m * ratio * S5_MXU_TILE, (m + 1) * ratio * S5_MXU_TILE)
        y_tiles.append(
            jnp.dot(xr_ref[:, states].astype(BF16), cre_ref[states, chans], preferred_element_type=F32)
            - jnp.dot(xi_ref[:, states].astype(BF16), cim_ref[states, chans], preferred_element_type=F32))
    y = jnp.concatenate(y_tiles, axis=1)
    y = jax.nn.gelu(y + d_ref[...] * u)
    z = jnp.dot(y.astype(BF16), gw_ref[...], preferred_element_type=F32) + gb_ref[...]
    o_ref[0] = y * jax.nn.sigmoid(z)


def _s5_params(a_re, a_im, log_dt, b_re, b_im, c_re, c_im):
    dt = jnp.exp(log_dt)[:, None]

    def apow(n):
        mag = jnp.exp(n * dt * a_re)
        return mag * jnp.cos(n * dt * a_im), mag * jnp.sin(n * dt * a_im)

    abar_re, abar_im = apow(1.0)
    den = a_re * a_re + a_im * a_im
    xr, xi = abar_re - 1.0, abar_im
    f_re = (xr * a_re + xi * a_im) / den
    f_im = (xi * a_re - xr * a_im) / den
    bbar_re = f_re[..., None] * b_re - f_im[..., None] * b_im
    bbar_im = f_re[..., None] * b_im + f_im[..., None] * b_re
    eye = jnp.eye(S5_GROUPS, dtype=F32)

    def blockdiag_in(b):
        return jnp.einsum('gpc,gh->gchp', b, eye).reshape(S5_WIDTH, S5_NSTATE)

    def blockdiag_out(c):
        return jnp.einsum('gcp,gh->gphc', c, eye).reshape(S5_NSTATE, S5_WIDTH)

    row = jnp.arange(SUBLANES)[:, None]
    step_r, step_i = [], []
    for k in range(3):
        pr, pi = apow(float(1 << k))
        keep = row >= (1 << k)
        step_r.append(jnp.where(keep, pr.reshape(1, -1), 0.0))
        step_i.append(jnp.where(keep, pi.reshape(1, -1), 0.0))
    car = [apow(float(r + 1)) for r in range(SUBLANES)]
    car_r = jnp.stack([p[0].reshape(-1) for p in car])
    car_i = jnp.stack([p[1].reshape(-1) for p in car])
    return (blockdiag_in(bbar_re).astype(BF16), blockdiag_in(bbar_im).astype(BF16),
            blockdiag_out(c_re).astype(BF16), blockdiag_out(c_im).astype(BF16),
            jnp.stack(step_r), jnp.stack(step_i), car_r, car_i)


def s5_mixer(proj, a_re, a_im, log_dt, b_re, b_im, c_re, c_im, d, glu_w, glu_b, *, ts):
    bsz, seq, _ = proj.shape
    bre, bim, cre, cim, step_r, step_i, car_r, car_i = _s5_params(a_re, a_im, log_dt, b_re, b_im, c_re, c_im)
    full = lambda shape: pl.BlockSpec(shape, lambda b, s: (0,) * len(shape))
    return pl.pallas_call(
        _s5_kernel,
        grid=(bsz, seq // ts),
        in_specs=[
            pl.BlockSpec((1, ts, S5_WIDTH), lambda b, s: (b, s, 0)),
            full((S5_WIDTH, S5_NSTATE)), full((S5_WIDTH, S5_NSTATE)),
            full((S5_NSTATE, S5_WIDTH)), full((S5_NSTATE, S5_WIDTH)),
            full((3, SUBLANES, S5_NSTATE)), full((3, SUBLANES, S5_NSTATE)),
            full((SUBLANES, S5_NSTATE)), full((SUBLANES, S5_NSTATE)),
            full((1, S5_WIDTH)), full((S5_WIDTH, S5_WIDTH)), full((1, S5_WIDTH)),
        ],
        out_specs=pl.BlockSpec((1, ts, S5_WIDTH), lambda b, s: (b, s, 0)),
        out_shape=jax.ShapeDtypeStruct((bsz, seq, S5_WIDTH), F32),
        scratch_shapes=[pltpu.VMEM((ts, S5_NSTATE), F32), pltpu.VMEM((ts, S5_NSTATE), F32),
                        pltpu.VMEM((1, S5_NSTATE), F32), pltpu.VMEM((1, S5_NSTATE), F32)],
        compiler_params=_cparams(("parallel", "arbitrary")),
        name="s5_mixer",
    )(proj, bre, bim, cre, cim, step_r, step_i, car_r, car_i,
      d.reshape(1, -1), glu_w.astype(BF16), glu_b.reshape(1, -1))


_NT = (((1,), (1,)), ((), ()))
_NN = (((1,), (0,)), ((), ()))
_TN = (((0,), (0,)), ((), ()))


def _split(x):
    hi = x.astype(BF16)
    return hi, (x - hi.astype(F32)).astype(BF16)


def _mm(a, b, dims=_NN):
    return lax.dot_general(a.astype(BF16), b.astype(BF16), dims, preferred_element_type=F32)


def _mm3(a, b, dims=_NN):
    ah, al = _split(a)
    bh, bl = _split(b)
    out = lax.dot_general(ah, bh, dims, preferred_element_type=F32)
    out += lax.dot_general(ah, bl, dims, preferred_element_type=F32)
    out += lax.dot_general(al, bh, dims, preferred_element_type=F32)
    return out


def _mm_exact_rhs(a, b01, dims=_NN):
    ah, al = _split(a)
    out = lax.dot_general(ah, b01, dims, preferred_element_type=F32)
    out += lax.dot_general(al, b01, dims, preferred_element_type=F32)
    return out


def _moba_kernel(q_ref, k_ref, v_ref, o_ref, kmean_ref, kb_ref, vt_ref, bias_ref):
    i = pl.program_id(2)
    blk = MOBA_BLOCK
    seq = k_ref.shape[1]
    nb = seq // blk
    nh = LANES // HEAD_DIM

    @pl.when(i == 0)
    def _():
        k = k_ref[0]
        kmean_ref[...] = jnp.mean(k.reshape(nb, blk, LANES), axis=1)
        kb_ref[...] = k.astype(BF16)
        vt_ref[...] = v_ref[0].T.astype(BF16)

    q = q_ref[0]
    lane = lax.broadcasted_iota(jnp.int32, (blk, LANES), 1)
    brow = lax.broadcasted_iota(jnp.int32, (nb, blk), 0)
    krow = lax.broadcasted_iota(jnp.int32, (blk, blk), 0)
    qcol = lax.broadcasted_iota(jnp.int32, (blk, blk), 1)
    own = pl.ds(pl.multiple_of(i * blk, blk), blk)
    k_own, vt_own = kb_ref[own, :], vt_ref[:, own]
    qbs, state = [], []
    for hh in range(nh):
        qh = jnp.where(lane // HEAD_DIM == hh, q, 0.0)
        work = jnp.where(brow < i, _mm3(kmean_ref[...], qh, _NT), -jnp.inf)
        sel = jnp.zeros((nb, blk), F32)
        for _ in range(MOBA_TOPK):
            m = jnp.max(work, axis=0, keepdims=True)
            first = jnp.min(jnp.where(work == m, brow, nb), axis=0, keepdims=True)
            pick = (brow == first) & (m > -jnp.inf)
            sel = jnp.where(pick, 1.0, sel)
            work = jnp.where(pick, -jnp.inf, work)
        bias_ref[hh] = jnp.where(sel > 0.0, 0.0, NEG_BIG)

        qb = (qh * (HEAD_DIM ** -0.5 * LOG2_E)).astype(BF16)
        s = lax.dot_general(k_own, qb, _NT, preferred_element_type=F32)
        s = jnp.where(krow <= qcol, s, NEG_BIG)
        m0 = jnp.max(s, axis=0, keepdims=True)
        p = jnp.exp2(s - m0)
        acc0 = jnp.dot(vt_own, p.astype(BF16), preferred_element_type=F32)
        qbs.append(qb)
        state += [m0, jnp.sum(p, axis=0, keepdims=True), acc0]

    grp, ns = MOBA_BLOCKS_PER_STEP, MOBA_STREAMS
    sub = grp // ns
    for _ in range(1, ns):
        for hh in range(nh):
            state += [jnp.full((1, blk), NEG_BIG, F32), jnp.zeros((1, blk), F32), jnp.zeros((LANES, blk), F32)]

    def body(t, carry):
        out = []
        for sidx in range(ns):
            first = t * grp + sidx * sub
            rows = pl.ds(pl.multiple_of(first * blk, sub * blk), sub * blk)
            kj, vtj = kb_ref[rows, :], vt_ref[:, rows]
            for hh in range(nh):
                m_prev, l_prev, acc = carry[3 * (sidx * nh + hh):3 * (sidx * nh + hh) + 3]
                s = lax.dot_general(kj, qbs[hh], _NT, preferred_element_type=F32)
                s = jnp.concatenate([s[g * blk:(g + 1) * blk] + bias_ref[hh, pl.ds(first + g, 1), :]
                                     for g in range(sub)], axis=0)
                m_new = jnp.maximum(m_prev, jnp.max(s, axis=0, keepdims=True))
                alpha = jnp.exp2(m_prev - m_new)
                p = jnp.exp2(s - m_new)
                out += [m_new, alpha * l_prev + jnp.sum(p, axis=0, keepdims=True),
                        alpha * acc + jnp.dot(vtj, p.astype(BF16), preferred_element_type=F32)]
        return tuple(out)

    fin = lax.fori_loop(0, (i + grp - 1) // grp, body, tuple(state))
    halves = []
    for hh in range(nh):
        parts = [fin[3 * (sidx * nh + hh):3 * (sidx * nh + hh) + 3] for sidx in range(ns)]
        m_fin = parts[0][0]
        for m_s, _, _ in parts[1:]:
            m_fin = jnp.maximum(m_fin, m_s)
        l_fin = sum(jnp.exp2(m_s - m_fin) * l_s for m_s, l_s, _ in parts)
        acc = sum(jnp.exp2(m_s - m_fin) * a_s for m_s, _, a_s in parts)
        halves.append(acc[hh * HEAD_DIM:(hh + 1) * HEAD_DIM] / l_fin)
    o_ref[0] = jnp.concatenate(halves, axis=0).T


def moba_attention(proj, *, q_col, k_col, v_col):
    bsz, seq, _ = proj.shape
    assert seq % (MOBA_BLOCK * MOBA_BLOCKS_PER_STEP) == 0
    nb = seq // MOBA_BLOCK
    qb, kb, vb = q_col // LANES, k_col // LANES, v_col // LANES
    return pl.pallas_call(
        _moba_kernel,
        grid=(bsz, MOBA_WIDTH // LANES, nb),
        in_specs=[
            pl.BlockSpec((1, MOBA_BLOCK, LANES), lambda b, p, i: (b, i, qb + p)),
            pl.BlockSpec((1, seq, LANES), lambda b, p, i: (b, 0, kb + p)),
            pl.BlockSpec((1, seq, LANES), lambda b, p, i: (b, 0, vb + p)),
        ],
        out_specs=pl.BlockSpec((1, MOBA_BLOCK, LANES), lambda b, p, i: (b, i, p)),
        out_shape=jax.ShapeDtypeStruct((bsz, seq, MOBA_WIDTH), F32),
        scratch_shapes=[pltpu.VMEM((nb, LANES), F32), pltpu.VMEM((seq, LANES), BF16),
                        pltpu.VMEM((LANES, seq), BF16),
                        pltpu.VMEM((LANES // HEAD_DIM, nb, MOBA_BLOCK), F32)],
        compiler_params=_cparams(("parallel", "parallel", "arbitrary")),
        name="moba_attention",
    )(proj, proj, proj)


def _head_sum_matrix(width):
    idx = np.arange(width) // HEAD_DIM
    return jnp.asarray((idx[:, None] == idx[None, :]).astype(np.float32)).astype(BF16)


def _rwkv_prep_kernel(p_ref, prev_ref, mu_ref, w0_ref, wup_ref, a0_ref, aup_ref, gup_ref, kk_ref, ka_ref,
                      rk_ref, hsum_ref, r_out, lw_out, k_out, v_out, kk_out, b_out, g_out, bonus_out):
    s = pl.program_id(1)
    p = p_ref[0]
    ts = p.shape[0]
    row = lax.broadcasted_iota(jnp.int32, p.shape, 0)
    last = jnp.where(s == 0, 0.0, 1.0) * prev_ref[0, SUBLANES - 1:SUBLANES, :]
    prev = jnp.where(row == 0, last, pltpu.roll(p, 1, axis=0))
    p = p + (prev - p) * mu_ref[...]
    w = RWKV_WIDTH
    r, k, v = p[:, 0:w], p[:, w:2 * w], p[:, 2 * w:3 * w]
    lora = p[:, 3 * w:3 * w + LANES]
    gd = p[:, 3 * w + LANES:3 * w + 2 * LANES]
    w_log = -jax.nn.softplus(-(w0_ref[...] + _mm3(jnp.tanh(lora), wup_ref[...]))) - 0.5
    lw_out[0] = -jnp.exp(w_log)
    a = jax.nn.sigmoid(a0_ref[...] + _mm3(lora, aup_ref[...]))
    g_out[0] = _mm(jax.nn.sigmoid(gd), gup_ref[...])
    kk = k * kk_ref[...]
    norm = jnp.sqrt(_mm_exact_rhs(kk * kk, hsum_ref[...]))
    kk = kk / jnp.maximum(norm, 1e-12)
    k = k * (1.0 + (a - 1.0) * ka_ref[...])
    r_out[0] = r
    k_out[0] = k
    v_out[0] = v
    kk_out[0] = kk
    b_out[0] = kk * a
    bonus_out[0] = _mm_exact_rhs(r * k * rk_ref[...], hsum_ref[...]) * v


RWKV_GROUP = 256


def _stack_heads(x):
    lane = lax.broadcasted_iota(jnp.int32, x.shape, 1)
    return jnp.concatenate([jnp.where(lane // HEAD_DIM == hh, x, 0.0)
                            for hh in range(RWKV_GROUP // HEAD_DIM)], axis=0)


def _rwkv_scan_kernel(r_ref, lw_ref, k_ref, v_ref, kk_ref, b_ref, g_ref, bonus_ref, lnw_ref, lnb_ref,
                      havg_ref, o_ref, y_ref, st_ref):
    s = pl.program_id(1)
    ts = r_ref.shape[1]
    cl_ = RWKV_CHUNK
    n = RWKV_GROUP
    ngroups = RWKV_WIDTH // RWKV_GROUP

    @pl.when(s == 0)
    def _():
        st_ref[...] = jnp.zeros_like(st_ref)

    ri = lax.broadcasted_iota(jnp.int32, (n, n), 0)
    ci = lax.broadcasted_iota(jnp.int32, (n, n), 1)
    same = (ri // cl_) == (ci // cl_)
    strict = same & ((ci % cl_) < (ri % cl_))
    incl = same & ((ci % cl_) <= (ri % cl_))
    eye = ri == ci
    tr = lax.broadcasted_iota(jnp.int32, (cl_, cl_), 0)
    tc = lax.broadcasted_iota(jnp.int32, (cl_, cl_), 1)
    cum = jnp.where(tc <= tr, 1.0, 0.0).astype(BF16)

    def chunk(c, _):
        rows = pl.ds(pl.multiple_of(c * cl_, cl_), cl_)
        lw = lw_ref[0, rows, :]
        lw_hi, lw_lo = _split(lw)
        cl_in = (jnp.dot(cum, lw_hi, preferred_element_type=F32)
                 + jnp.dot(cum, lw_lo, preferred_element_type=F32))
        cl_ex = cl_in - lw
        tot = cl_in[cl_ - 1:cl_, :]
        e_in = jnp.exp(cl_in)
        e_ninv = jnp.exp(-cl_in)
        e_rem = jnp.exp(tot - cl_in)
        kk = kk_ref[0, rows, :]
        bb = b_ref[0, rows, :]
        k2 = k_ref[0, rows, :]
        a_t = -kk * jnp.exp(cl_ex)
        b_t = bb * e_ninv
        k_t = k2 * e_ninv
        r_t = r_ref[0, rows, :] * e_in
        b_g = bb * e_rem
        k_g = k2 * e_rem
        vv = v_ref[0, rows, :]
        gam = jnp.exp(tot)
        for grp in range(ngroups):
            ls = slice(grp * n, (grp + 1) * n)
            a_s, b_s, k_s, r_s = (_stack_heads(t[:, ls]) for t in (a_t, b_t, k_t, r_t))
            v_s, bg_s, kg_s = (_stack_heads(t[:, ls]) for t in (vv, b_g, k_g))
            gram = _mm(jnp.concatenate([a_s, r_s], axis=0), jnp.concatenate([b_s, k_s], axis=0), _NT)
            nmat = jnp.where(strict, gram[:n, :n], 0.0)
            m_ak = jnp.where(strict, gram[:n, n:], 0.0)
            m_rb = jnp.where(incl, gram[n:, :n], 0.0)
            m_rk = jnp.where(incl, gram[n:, n:], 0.0)
            winv = jnp.where(eye, 1.0, 0.0) + nmat
            pw = nmat
            for _ in range(int(math.log2(cl_)) - 1):
                pw = _mm(pw, pw)
                winv = winv + _mm(winv, pw)
            st = st_ref[grp]
            ar_st = _mm(jnp.concatenate([a_s, r_s], axis=0), st)
            u_s = _mm(winv, ar_st[:n] + _mm(m_ak, v_s))
            y_s = ar_st[n:] + _mm(m_rb, u_s) + _mm(m_rk, v_s)
            y4 = y_s[0:cl_]
            for hh in range(1, n // cl_):
                y4 = y4 + y_s[hh * cl_:(hh + 1) * cl_]
            y_ref[rows, ls] = y4
            colg = jnp.sum(jnp.where(eye, jnp.broadcast_to(gam[:, ls], (n, n)), 0.0), axis=1, keepdims=True)
            st_ref[grp] = colg * st + _mm(jnp.concatenate([bg_s, kg_s], axis=0),
                                          jnp.concatenate([u_s, v_s], axis=0), _TN)
        return 0

    lax.fori_loop(0, ts // cl_, chunk, 0)

    y = y_ref[...]
    mean = _mm_exact_rhs(y, havg_ref[...])
    yc = y - mean
    var = _mm_exact_rhs(yc * yc, havg_ref[...])
    y = yc * lax.rsqrt(var + RWKV_GN_EPS) * lnw_ref[...] + lnb_ref[...]
    o_ref[0] = (y + bonus_ref[0]) * g_ref[0]


def rwkv7_time_mix(proj, mu, w0, w_up, a0, a_up, g_up, k_k, k_a, r_k, ln_w, ln_b, *, ts):
    bsz, seq, _ = proj.shape
    w = RWKV_WIDTH
    pw = 3 * w + 2 * LANES
    assert RWKV_W_LORA + RWKV_A_LORA == LANES and RWKV_G_LORA == LANES and seq % ts == 0
    wup = jnp.concatenate([w_up, jnp.zeros((RWKV_A_LORA, w), F32)], axis=0)
    aup = jnp.concatenate([jnp.zeros((RWKV_W_LORA, w), F32), a_up], axis=0)
    hsum = _head_sum_matrix(w)
    row = lambda t: t.reshape(1, -1)
    full = lambda shape: pl.BlockSpec(shape, lambda b, s: (0,) * len(shape))
    tile = pl.BlockSpec((1, ts, w), lambda b, s: (b, s, 0))
    nprev = ts // SUBLANES
    outs = pl.pallas_call(
        _rwkv_prep_kernel,
        grid=(bsz, seq // ts),
        in_specs=[
            pl.BlockSpec((1, ts, pw), lambda b, s: (b, s, 0)),
            pl.BlockSpec((1, SUBLANES, pw), lambda b, s: (b, jnp.maximum(s * nprev - 1, 0), 0)),
            full((1, pw)), full((1, w)), full((LANES, w)), full((1, w)), full((LANES, w)),
            full((LANES, w)), full((1, w)), full((1, w)), full((1, w)), full((w, w)),
        ],
        out_specs=[tile] * 8,
        out_shape=[jax.ShapeDtypeStruct((bsz, seq, w), F32)] * 8,
        compiler_params=_cparams(("parallel", "parallel")),
        name="rwkv_prep",
    )(proj, proj, row(mu), row(w0), wup, row(a0), aup, g_up.astype(BF16), row(k_k), row(k_a),
      row(r_k), hsum)
    havg = (_head_sum_matrix(w).astype(F32) / HEAD_DIM).astype(BF16)
    return pl.pallas_call(
        _rwkv_scan_kernel,
        grid=(bsz, seq // ts),
        in_specs=[tile] * 8 + [full((1, w)), full((1, w)), full((w, w))],
        out_specs=tile,
        out_shape=jax.ShapeDtypeStruct((bsz, seq, w), F32),
        scratch_shapes=[pltpu.VMEM((ts, w), F32),
                        pltpu.VMEM((w // RWKV_GROUP, RWKV_GROUP, RWKV_GROUP), F32)],
        compiler_params=_cparams(("parallel", "arbitrary")),
        name="rwkv_scan",
    )(*outs, row(ln_w), row(ln_b), havg)


DSA_KV_TILE = 256
DSA_SEARCH_TILE = 512
DSA_COUNT_ROWS = 64
DSA_STREAMS = 2
DSA_VROWS = 128
INT32_MIN = -2 ** 31
NEG_INF_KEY = -2139095041


def _sortable_key(x, pos, npos):
    bits = pltpu.bitcast(x + 0.0, jnp.int32)
    key = jnp.where(bits < 0, bits ^ jnp.int32(0x7FFFFFFF), bits + npos)
    return jnp.where(bits == 0, npos - 1 - pos, key)


def _dsa_kernel(q_ref, qi_ref, wi_ref, k_ref, v_ref, ki_ref, o_ref,
                kb_ref, vt_ref, kib_ref, key_ref, acc_ref):
    i = pl.program_id(1)
    nq, kt, st = DSA_QBLOCK, DSA_KV_TILE, DSA_SEARCH_TILE
    seq = k_ref.shape[1]
    topk = min(DSA_TOPK, seq // 4)

    @pl.when(i == 0)
    def _():
        kb_ref[...] = k_ref[0].astype(BF16)
        vrow = lax.broadcasted_iota(jnp.int32, (DSA_VROWS, seq), 0)
        ones_row = jnp.where(vrow == HEAD_DIM, 1.0, 0.0)
        vt_ref[...] = jnp.where(vrow < HEAD_DIM, v_ref[0].T[:DSA_VROWS], ones_row).astype(BF16)
        kib_ref[...] = ki_ref[0].astype(BF16)
        key_ref[...] = jnp.full(key_ref.shape, NEG_INF_KEY, jnp.int32)

    n_search = (i * nq) // st + 1
    spos = i * nq + lax.broadcasted_iota(jnp.int32, (st, nq), 1)
    lane = lax.broadcasted_iota(jnp.int32, (nq, LANES), 1)
    qpos = i * nq + lax.broadcasted_iota(jnp.int32, (kt, nq), 1)
    krow = lax.broadcasted_iota(jnp.int32, (kt, nq), 0)
    srow = lax.broadcasted_iota(jnp.int32, (st, nq), 0)

    def stack_heads(ref, nheads):
        parts = []
        for h in range(nheads):
            pair = ref[0, :, (h // 2) * LANES:(h // 2 + 1) * LANES]
            parts.append(jnp.where(lane // HEAD_DIM == h % 2, pair, 0.0))
        return jnp.concatenate(parts, axis=0)

    qib = stack_heads(qi_ref, DSA_IDX_HEADS).astype(BF16)
    wi_t = (wi_ref[0] * (DSA_IDX_HEADS ** -0.5 * HEAD_DIM ** -0.5)).T
    w_lanes = jnp.concatenate([wi_t[h:h + 1, :] for h in range(DSA_IDX_HEADS)], axis=1)

    def score_tile(c, _):
        rows = pl.ds(pl.multiple_of(c * kt, kt), kt)
        logit = lax.dot_general(kib_ref[rows, :], qib, _NT, preferred_element_type=F32)
        weighted = jnp.maximum(logit, 0.0) * w_lanes
        score = weighted[:, 0:nq]
        for h in range(1, DSA_IDX_HEADS):
            score = score + weighted[:, h * nq:(h + 1) * nq]
        kpos = c * kt + krow
        score = jnp.where(kpos <= qpos, score, -jnp.inf)
        key_ref[rows, :] = _sortable_key(score, kpos, seq)
        return 0

    lax.fori_loop(0, n_search * (st // kt), score_tile, 0)

    def count(pred):
        def tile(c, acc):
            rows = pl.ds(pl.multiple_of(c * st, st), st)
            hit = jnp.where(pred(key_ref[rows, :], c * st + srow), 1.0, 0.0)
            for part in range(st // DSA_COUNT_ROWS):
                acc = acc + hit[part * DSA_COUNT_ROWS:(part + 1) * DSA_COUNT_ROWS]
            return acc
        acc = lax.fori_loop(0, n_search, tile, jnp.zeros((DSA_COUNT_ROWS, nq), F32))
        return jnp.sum(acc, axis=0, keepdims=True)

    def search_step(t, base):
        cand = base + lax.shift_left(jnp.int32(1), 31 - t)
        cnt = count(lambda key, pos: key >= cand)
        return jnp.where(cnt >= topk, cand, base)

    thr = lax.fori_loop(0, 32, search_step, jnp.full((1, nq), INT32_MIN, jnp.int32))
    need = topk - count(lambda key, pos: key > thr)
    n_eq = count(lambda key, pos: key == thr)
    tied = jnp.max(jnp.where((n_eq > need) & (thr > NEG_INF_KEY), 1.0, 0.0)) > 0.0
    nbits = seq.bit_length()

    def tie_search(_):
        def step(t, pos):
            cand = pos + lax.shift_left(jnp.int32(1), nbits - 1 - t)
            cnt = count(lambda key, kpos: (key == thr) & (kpos < cand))
            return jnp.where(cnt < need, cand, pos)
        return lax.fori_loop(0, nbits, step, jnp.zeros((1, nq), jnp.int32))

    last = lax.cond(tied, tie_search, lambda _: jnp.full((1, nq), seq, jnp.int32), 0)

    qs = (stack_heads(q_ref, DSA_HEADS) * (HEAD_DIM ** -0.5 * LOG2_E)).astype(BF16)
    acc_ref[...] = jnp.zeros_like(acc_ref)

    ns = DSA_STREAMS

    def attn_step(t, m_all):
        m_rows = []
        for sidx in range(ns):
            c = t * ns + sidx
            rows = pl.ds(pl.multiple_of(c * st, st), st)
            key = key_ref[rows, :]
            kpos = c * st + srow
            chosen = ((key > thr) | ((key == thr) & (kpos <= last))) & (kpos <= spos)
            bias = jnp.where(chosen, 0.0, NEG_BIG)
            kb, vt = kb_ref[rows, :], vt_ref[:, rows]
            for h in range(DSA_HEADS):
                slot = sidx * DSA_HEADS + h
                s = lax.dot_general(kb, qs[h * nq:(h + 1) * nq], _NT, preferred_element_type=F32) + bias
                m_prev = m_all[slot:slot + 1, :]
                m_new = jnp.maximum(m_prev, jnp.max(s, axis=0, keepdims=True))
                p = jnp.exp2(s - m_new).astype(BF16)
                acc_ref[slot] = (jnp.exp2(m_prev - m_new) * acc_ref[slot]
                                 + jnp.dot(vt, p, preferred_element_type=F32))
                m_rows.append(m_new)
        return jnp.concatenate(m_rows, axis=0)

    m_all = lax.fori_loop(0, (n_search + ns - 1) // ns, attn_step,
                          jnp.full((ns * DSA_HEADS, nq), NEG_BIG, F32))

    def head_output(h):
        slots = [sidx * DSA_HEADS + h for sidx in range(ns)]
        m_fin = m_all[slots[0]:slots[0] + 1, :]
        for slot in slots[1:]:
            m_fin = jnp.maximum(m_fin, m_all[slot:slot + 1, :])
        acc = sum(jnp.exp2(m_all[slot:slot + 1, :] - m_fin) * acc_ref[slot] for slot in slots)
        return acc[:HEAD_DIM] / acc[HEAD_DIM:HEAD_DIM + 1]

    for pr in range(DSA_WIDTH // LANES):
        halves = [head_output(2 * pr), head_output(2 * pr + 1)]
        o_ref[0, :, pr * LANES:(pr + 1) * LANES] = jnp.concatenate(halves, axis=0).T


def dsa_attention(proj, *, q_col, qi_col, wi_col, k_col, v_col, ki_col):
    bsz, seq, _ = proj.shape
    assert seq % DSA_SEARCH_TILE == 0 and DSA_SEARCH_TILE % DSA_KV_TILE == 0 and DSA_KV_TILE % DSA_QBLOCK == 0
    nq = seq // DSA_QBLOCK
    kvspec = lambda col: pl.BlockSpec((1, seq, LANES), lambda b, i: (b, 0, col // LANES))
    return pl.pallas_call(
        _dsa_kernel,
        grid=(bsz, nq),
        in_specs=[
            pl.BlockSpec((1, DSA_QBLOCK, DSA_WIDTH), lambda b, i: (b, i, q_col // DSA_WIDTH)),
            pl.BlockSpec((1, DSA_QBLOCK, 2 * LANES), lambda b, i: (b, i, qi_col // (2 * LANES))),
            pl.BlockSpec((1, DSA_QBLOCK, LANES), lambda b, i: (b, i, wi_col // LANES)),
            kvspec(k_col), kvspec(v_col), kvspec(ki_col),
        ],
        out_specs=pl.BlockSpec((1, DSA_QBLOCK, DSA_WIDTH), lambda b, i: (b, i, 0)),
        out_shape=jax.ShapeDtypeStruct((bsz, seq, DSA_WIDTH), F32),
        scratch_shapes=[pltpu.VMEM((seq, LANES), BF16), pltpu.VMEM((DSA_VROWS, seq), BF16),
                        pltpu.VMEM((seq, LANES), BF16),
                        pltpu.VMEM((seq, DSA_QBLOCK), jnp.int32),
                        pltpu.VMEM((DSA_STREAMS * DSA_HEADS, DSA_VROWS, DSA_QBLOCK), F32)],
        compiler_params=_cparams(("parallel", "arbitrary")),
        name="dsa_attention",
    )(proj, proj, proj, proj, proj, proj)


MOE_TOKEN_TILE = 2048
MOE_ROW_BLOCK = 256


def _router_kernel(h_ref, g_ref, wr_ref, xn_ref, route_ref):
    xn = _rms(h_ref[...], g_ref[...])
    xn_ref[...] = xn.astype(BF16)
    lane = lax.broadcasted_iota(jnp.int32, (xn.shape[0], LANES), 1)
    logits = jnp.where(lane < N_EXPERTS, _mm3(xn, wr_ref[...]), -jnp.inf)
    v1 = jnp.max(logits, axis=1, keepdims=True)
    e1 = jnp.min(jnp.where(logits == v1, lane, LANES), axis=1, keepdims=True)
    rest = jnp.where(lane == e1, -jnp.inf, logits)
    v2 = jnp.max(rest, axis=1, keepdims=True)
    e2 = jnp.min(jnp.where(rest == v2, lane, LANES), axis=1, keepdims=True)
    ratio = jnp.exp(v2 - v1)
    g1 = 1.0 / (1.0 + ratio)
    g2 = ratio * g1
    route_ref[...] = jnp.where(lane == 0, e1.astype(F32), jnp.where(
        lane == 1, e2.astype(F32), jnp.where(lane == 2, g1, jnp.where(lane == 3, g2, 0.0))))


def route_tokens(h2d, g, router, *, tm):
    t, d = h2d.shape
    wr = jnp.zeros((d, LANES), F32).at[:, :N_EXPERTS].set(router)
    return pl.pallas_call(
        _router_kernel,
        grid=(t // tm,),
        in_specs=[pl.BlockSpec((tm, d), lambda i: (i, 0)), pl.BlockSpec((1, d), lambda i: (0, 0)),
                  pl.BlockSpec((d, LANES), lambda i: (0, 0))],
        out_specs=[pl.BlockSpec((tm, d), lambda i: (i, 0)), pl.BlockSpec((tm, LANES), lambda i: (i, 0))],
        out_shape=[jax.ShapeDtypeStruct((t, d), BF16), jax.ShapeDtypeStruct((t, LANES), F32)],
        compiler_params=_cparams(("parallel",)),
        name="moe_router",
    )(h2d, g.reshape(1, d), wr)


def _moe_kernel(cnt_ref, xn_ref, rk_ref, gt_ref, rkt_ref, wg_ref, wu_ref, wd_ref, o_ref, xs_ref, acc_ref):
    tau, e, f = pl.program_id(0), pl.program_id(1), pl.program_id(2)
    tt = xn_ref.shape[0]
    rb = MOE_ROW_BLOCK
    nblk = (cnt_ref[tau * N_EXPERTS + e] + rb - 1) // rb
    slot = lax.broadcasted_iota(jnp.int32, (rb, tt), 0)

    @pl.when((e == 0) & (f == 0))
    def _():
        o_ref[...] = jnp.zeros_like(o_ref)

    def rows_of(b):
        return pl.ds(pl.multiple_of(b * rb, rb), rb)

    @pl.when(f == 0)
    def _():
        rank = rk_ref[0, pl.ds(e, 1), :]

        def gather(b, _):
            onehot = jnp.where(rank == slot + b * rb, 1.0, 0.0).astype(BF16)
            xs_ref[rows_of(b), :] = jnp.dot(onehot, xn_ref[...], preferred_element_type=F32).astype(BF16)
            acc_ref[rows_of(b), :] = jnp.zeros((rb, acc_ref.shape[1]), F32)
            return 0

        lax.fori_loop(0, nblk, gather, 0)

    def ffn(b, _):
        xs = xs_ref[rows_of(b), :]
        gate = jnp.dot(xs, wg_ref[0], preferred_element_type=F32)
        up = jnp.dot(xs, wu_ref[0], preferred_element_type=F32)
        act = (gate * jax.nn.sigmoid(gate) * up).astype(BF16)
        acc_ref[rows_of(b), :] += jnp.dot(act, wd_ref[0], preferred_element_type=F32)
        return 0

    lax.fori_loop(0, nblk, ffn, 0)

    @pl.when(f == pl.num_programs(2) - 1)
    def _():
        rank = rk_ref[0, pl.ds(e, 1), :]
        gates = gt_ref[0, pl.ds(e, 1), :]
        lane_e = lax.broadcasted_iota(jnp.int32, rkt_ref.shape[1:], 1)
        rank_t = jnp.sum(jnp.where(lane_e == e, rkt_ref[0], 0), axis=1, keepdims=True)
        slot_t = lax.broadcasted_iota(jnp.int32, (tt, rb), 1)

        def combine(b, _):
            hit = rank == slot + b * rb
            gate_slot = jnp.sum(jnp.where(hit, gates, 0.0), axis=1, keepdims=True)
            yg = (acc_ref[rows_of(b), :] * gate_slot).astype(BF16)
            scatter = jnp.where(rank_t == slot_t + b * rb, 1.0, 0.0).astype(BF16)
            o_ref[...] += jnp.dot(scatter, yg, preferred_element_type=F32)
            return 0

        lax.fori_loop(0, nblk, combine, 0)


def moe_experts(xn_bf16, route, w_gate, w_up, w_down, *, tf):
    t, d = xn_bf16.shape
    tt = MOE_TOKEN_TILE
    ntile = t // tt
    fdim = w_gate.shape[2]
    experts = route[:, 0:2].astype(jnp.int32).reshape(ntile, tt * 2)
    gates = route[:, 2:4].reshape(ntile, tt * 2)
    onehot = (experts[:, :, None] == jnp.arange(N_EXPERTS)[None, None, :]).astype(jnp.int32)
    csum = jnp.cumsum(onehot, axis=1)
    counts = csum[:, -1, :]
    rank = jnp.where(onehot > 0, csum - 1, -1)
    rank_tok = jnp.max(rank.reshape(ntile, tt, 2, N_EXPERTS), axis=2)
    gate_tok = jnp.sum((onehot * gates[:, :, None]).reshape(ntile, tt, 2, N_EXPERTS), axis=2)
    rk = jnp.swapaxes(rank_tok, 1, 2)
    gt = jnp.swapaxes(gate_tok, 1, 2)
    grid_spec = pltpu.PrefetchScalarGridSpec(
        num_scalar_prefetch=1,
        grid=(ntile, N_EXPERTS, fdim // tf),
        in_specs=[
            pl.BlockSpec((tt, d), lambda i, e, f, c: (i, 0)),
            pl.BlockSpec((1, N_EXPERTS, tt), lambda i, e, f, c: (i, 0, 0)),
            pl.BlockSpec((1, N_EXPERTS, tt), lambda i, e, f, c: (i, 0, 0)),
            pl.BlockSpec((1, tt, N_EXPERTS), lambda i, e, f, c: (i, 0, 0)),
            pl.BlockSpec((1, d, tf), lambda i, e, f, c: (e, 0, f)),
            pl.BlockSpec((1, d, tf), lambda i, e, f, c: (e, 0, f)),
            pl.BlockSpec((1, tf, d), lambda i, e, f, c: (e, f, 0)),
        ],
        out_specs=pl.BlockSpec((tt, d), lambda i, e, f, c: (i, 0)),
        scratch_shapes=[pltpu.VMEM((tt, d), BF16), pltpu.VMEM((tt, d), F32)],
    )
    return pl.pallas_call(
        _moe_kernel,
        grid_spec=grid_spec,
        out_shape=jax.ShapeDtypeStruct((t, d), F32),
        compiler_params=_cparams(("parallel", "arbitrary", "arbitrary")),
        name="moe_experts",
    )(counts.reshape(-1), xn_bf16, rk, gt, rank_tok, w_gate, w_up, w_down)


def _add_norm_kernel(a_ref, b_ref, g_ref, o_ref, *, normalize):
    y = a_ref[...] + b_ref[...]
    o_ref[...] = _rms(y, g_ref[...]) if normalize else y


def add_rmsnorm(a, b, g, *, tm):
    t, d = a.shape
    spec = pl.BlockSpec((tm, d), lambda i: (i, 0))
    gain = jnp.ones((1, d), F32) if g is None else g.reshape(1, d)
    return pl.pallas_call(
        functools.partial(_add_norm_kernel, normalize=g is not None),
        grid=(t // tm,),
        in_specs=[spec, spec, pl.BlockSpec((1, d), lambda i: (0, 0))],
        out_specs=spec,
        out_shape=jax.ShapeDtypeStruct((t, d), F32),
        compiler_params=_cparams(("parallel",)),
        name="add_rmsnorm",
    )(a, b, gain)


ROW_TILE = 512
FFN_ROW_TILE = 1024
FFN_COL_TILE = 256

EVEN_Q_COL, EVEN_K_COL, EVEN_V_COL = 512, 1024, 1536
EVEN_ROPE_COLS = np.zeros(2048, bool)
EVEN_ROPE_COLS[EVEN_Q_COL:EVEN_V_COL] = True


def _odd_layout():
    rw = 3 * RWKV_WIDTH + RWKV_W_LORA + RWKV_A_LORA + RWKV_G_LORA
    q0 = rw
    k0 = q0 + DSA_WIDTH
    v0 = k0 + HEAD_DIM
    qi0 = v0 + HEAD_DIM
    ki0 = qi0 + DSA_IDX_HEADS * HEAD_DIM
    wi0 = ki0 + HEAD_DIM
    seg = lambda start, n: list(range(start, start + n))
    src, keep, rope = [], [], []

    def add(cols, roped, pad=0):
        src.extend(cols + [0] * pad)
        keep.extend([1.0] * len(cols) + [0.0] * pad)
        rope.extend([roped] * (len(cols) + pad))

    add(seg(0, rw), False)
    cols = {"k_col": len(src)}
    add(seg(k0, HEAD_DIM) * 2, True)
    cols["v_col"] = len(src)
    add(seg(v0, HEAD_DIM) * 2, False)
    cols["q_col"] = len(src)
    add(seg(q0, DSA_WIDTH), True)
    cols["qi_col"] = len(src)
    add(seg(qi0, DSA_IDX_HEADS * HEAD_DIM), True)
    cols["ki_col"] = len(src)
    add(seg(ki0, HEAD_DIM) * 2, True)
    cols["wi_col"] = len(src)
    add(seg(wi0, DSA_IDX_HEADS), False, pad=LANES - DSA_IDX_HEADS)
    return (np.asarray(src, np.int32), np.asarray(keep, np.float32), np.asarray(rope, bool), cols)


ODD_COL_SRC, ODD_COL_KEEP, ODD_ROPE_COLS, ODD_DSA_COLS = _odd_layout()


def kernel(x, e_norm_mix, e_w_in, s5_a_re, s5_a_im, s5_log_dt, s5_b_re, s5_b_im, s5_c_re, s5_c_im, s5_d, s5_glu_w, s5_glu_b, e_w_out, e_norm_ffn, ffn_w_gate, ffn_w_up, ffn_w_down, o_norm_mix, o_w_in, rwkv_mu, rwkv_w0, rwkv_w_up, rwkv_a0, rwkv_a_up, rwkv_g_up, rwkv_k_k, rwkv_k_a, rwkv_r_k, rwkv_ln_w, rwkv_ln_b, o_w_out, o_norm_ffn, moe_router, moe_w_gate, moe_w_up, moe_w_down, final_norm):
    bsz, seq, d = x.shape
    n_even, n_odd = e_norm_mix.shape[0], o_norm_mix.shape[0]
    h = x.reshape(-1, d)
    zero = jnp.zeros((bsz * seq, d), F32)
    for layer in range(n_even + n_odd):
        j = layer // 2
        last = layer == n_even + n_odd - 1
        if layer % 2 == 0:
            proj = norm_matmul_rope(h, e_norm_mix[j], e_w_in[j].astype(BF16), EVEN_ROPE_COLS, seq,
                                    tm=ROW_TILE, tn=512).reshape(bsz, seq, -1)
            ya = s5_mixer(proj, s5_a_re[j], s5_a_im[j], s5_log_dt[j], s5_b_re[j], s5_b_im[j], s5_c_re[j],
                          s5_c_im[j], s5_d[j], s5_glu_w[j], s5_glu_b[j], ts=256)
            yb = moba_attention(proj, q_col=EVEN_Q_COL, k_col=EVEN_K_COL, v_col=EVEN_V_COL)
            h = out_proj_residual(h, ya.reshape(-1, S5_WIDTH), yb.reshape(-1, MOBA_WIDTH),
                                  e_w_out[j].astype(BF16), tm=ROW_TILE)
            h = dense_ffn_residual(h, e_norm_ffn[j], ffn_w_gate[j].astype(BF16), ffn_w_up[j].astype(BF16),
                                   ffn_w_down[j].astype(BF16), tm=FFN_ROW_TILE, tf=FFN_COL_TILE)
            if last:
                h = add_rmsnorm(h, zero, final_norm, tm=ROW_TILE)
        else:
            w_in = jnp.take(o_w_in[j], jnp.asarray(ODD_COL_SRC), axis=1) * jnp.asarray(ODD_COL_KEEP)[None, :]
            proj = norm_matmul_rope(h, o_norm_mix[j], w_in.astype(BF16), ODD_ROPE_COLS, seq,
                                    tm=ROW_TILE, tn=512).reshape(bsz, seq, -1)
            yc = rwkv7_time_mix(proj, rwkv_mu[j], rwkv_w0[j], rwkv_w_up[j], rwkv_a0[j], rwkv_a_up[j],
                                rwkv_g_up[j], rwkv_k_k[j], rwkv_k_a[j], rwkv_r_k[j], rwkv_ln_w[j],
                                rwkv_ln_b[j], ts=256)
            yd = dsa_attention(proj, **ODD_DSA_COLS)
            h = out_proj_residual(h, yc.reshape(-1, RWKV_WIDTH), yd.reshape(-1, DSA_WIDTH),
                                  o_w_out[j].astype(BF16), tm=ROW_TILE)
            xn, route = route_tokens(h, o_norm_ffn[j], moe_router[j], tm=ROW_TILE)
            y = moe_experts(xn, route, moe_w_gate[j].astype(BF16), moe_w_up[j].astype(BF16),
                            moe_w_down[j].astype(BF16), tf=512)
            h = add_rmsnorm(h, y, final_norm if last else None, tm=ROW_TILE)
    return h.reshape(bsz, seq, d)
```

```python
import functools
import math

import jax
import jax.numpy as jnp
import numpy as np
from jax import lax
from jax.experimental import pallas as pl
from jax.experimental.pallas import tpu as pltpu

F32 = jnp.float32
BF16 = jnp.bfloat16
HIGHEST = lax.Precision.HIGHEST

LANES = 128
SUBLANES = 8
VMEM_LIMIT_BYTES = 56 * 1024 * 1024

D_MODEL = 1024
HEAD_DIM = 64
ROT_DIM = HEAD_DIM // 4
ROPE_THETA = 500000.0
NORM_EPS = 1e-6
S5_WIDTH = 512
S5_GROUP = 16
S5_GROUPS = S5_WIDTH // S5_GROUP
S5_STATE = 64
S5_NSTATE = S5_GROUPS * S5_STATE
MOBA_WIDTH = 512
MOBA_HEADS = MOBA_WIDTH // HEAD_DIM
MOBA_BLOCK = 256
MOBA_TOPK = 3
MOBA_BLOCKS_PER_STEP = 4
MOBA_STREAMS = 1
RWKV_WIDTH = 512
RWKV_HEADS = RWKV_WIDTH // HEAD_DIM
RWKV_W_LORA = 64
RWKV_A_LORA = 64
RWKV_G_LORA = 128
RWKV_GN_EPS = 64e-5
RWKV_CHUNK = 64
DSA_WIDTH = 512
DSA_HEADS = DSA_WIDTH // HEAD_DIM
DSA_IDX_HEADS = 4
DSA_TOPK = 256
DSA_QBLOCK = 128
N_EXPERTS = 8
NEG_BIG = -1e30
LOG2_E = 1.4426950408889634


def _cparams(sem):
    return pltpu.CompilerParams(dimension_semantics=sem, vmem_limit_bytes=VMEM_LIMIT_BYTES)


def _rms(x, g):
    return x * lax.rsqrt(jnp.mean(x * x, axis=-1, keepdims=True) + NORM_EPS) * g


def _norm_matmul_kernel(flags_ref, h_ref, g_ref, w_ref, cos_ref, sina_ref, sinb_ref, colf_ref,
                        o_ref, xn_ref):
    j = pl.program_id(1)

    @pl.when(j == 0)
    def _():
        xn_ref[...] = _rms(h_ref[...], g_ref[...]).astype(BF16)

    acc = jnp.dot(xn_ref[...], w_ref[...], preferred_element_type=F32)
    tn = acc.shape[1]

    @pl.when(flags_ref[j] == 0)
    def _():
        o_ref[...] = acc

    @pl.when(flags_ref[j] != 0)
    def _():
        reps = tn // LANES
        colf = colf_ref[...]
        cos = 1.0 + colf * (jnp.tile(cos_ref[...], (1, reps)) - 1.0)
        sina = colf * jnp.tile(sina_ref[...], (1, reps))
        sinb = colf * jnp.tile(sinb_ref[...], (1, reps))
        nxt = pltpu.roll(acc, tn - ROT_DIM // 2, axis=1)
        prv = pltpu.roll(acc, ROT_DIM // 2, axis=1)
        o_ref[...] = acc * cos + nxt * sina + prv * sinb


def _rope_tables(seq):
    half = ROT_DIM // 2
    inv_freq = ROPE_THETA ** (-jnp.arange(half, dtype=F32) / half)
    ang = jnp.arange(seq, dtype=F32)[:, None] * inv_freq[None, :]
    cos, sin = jnp.cos(ang), jnp.sin(ang)
    ones = jnp.ones((seq, HEAD_DIM - ROT_DIM), F32)
    zeros = jnp.zeros((seq, HEAD_DIM - ROT_DIM), F32)
    zh = jnp.zeros((seq, half), F32)
    cos64 = jnp.concatenate([cos, cos, ones], axis=1)
    sina64 = jnp.concatenate([-sin, zh, zeros], axis=1)
    sinb64 = jnp.concatenate([zh, sin, zeros], axis=1)
    rep = LANES // HEAD_DIM
    return jnp.tile(cos64, (1, rep)), jnp.tile(sina64, (1, rep)), jnp.tile(sinb64, (1, rep))


def norm_matmul_rope(h2d, g, w_bf16, rope_cols, seq, *, tm, tn):
    t, d = h2d.shape
    n = w_bf16.shape[1]
    assert t % tm == 0 and n % tn == 0 and seq % tm == 0 and tn % LANES == 0
    nseq = seq // tm
    cos, sina, sinb = _rope_tables(seq)
    tile_flags = jnp.asarray(rope_cols.reshape(n // tn, tn).any(axis=1).astype(np.int32))
    colf = jnp.asarray(rope_cols.astype(np.float32))[None, :]
    grid_spec = pltpu.PrefetchScalarGridSpec(
        num_scalar_prefetch=1,
        grid=(t // tm, n // tn),
        in_specs=[
            pl.BlockSpec((tm, d), lambda i, j, f: (i, 0)),
            pl.BlockSpec((1, d), lambda i, j, f: (0, 0)),
            pl.BlockSpec((d, tn), lambda i, j, f: (0, j)),
            pl.BlockSpec((tm, LANES), lambda i, j, f: (i % nseq, 0)),
            pl.BlockSpec((tm, LANES), lambda i, j, f: (i % nseq, 0)),
            pl.BlockSpec((tm, LANES), lambda i, j, f: (i % nseq, 0)),
            pl.BlockSpec((1, tn), lambda i, j, f: (0, j)),
        ],
        out_specs=pl.BlockSpec((tm, tn), lambda i, j, f: (i, j)),
        scratch_shapes=[pltpu.VMEM((tm, d), BF16)],
    )
    return pl.pallas_call(
        _norm_matmul_kernel,
        grid_spec=grid_spec,
        out_shape=jax.ShapeDtypeStruct((t, n), F32),
        compiler_params=_cparams(("parallel", "arbitrary")),
        name="norm_matmul_rope",
    )(tile_flags, h2d, g.reshape(1, d), w_bf16, cos, sina, sinb, colf)


def _out_proj_kernel(res_ref, a_ref, b_ref, wa_ref, wb_ref, o_ref):
    acc = jnp.dot(a_ref[...].astype(BF16), wa_ref[...], preferred_element_type=F32)
    acc += jnp.dot(b_ref[...].astype(BF16), wb_ref[...], preferred_element_type=F32)
    o_ref[...] = res_ref[...] + acc


def out_proj_residual(res, a, b, w_bf16, *, tm):
    t, d = res.shape
    ka, kb = a.shape[1], b.shape[1]
    wa, wb = w_bf16[:ka], w_bf16[ka:]
    return pl.pallas_call(
        _out_proj_kernel,
        grid=(t // tm,),
        in_specs=[
            pl.BlockSpec((tm, d), lambda i: (i, 0)),
            pl.BlockSpec((tm, ka), lambda i: (i, 0)),
            pl.BlockSpec((tm, kb), lambda i: (i, 0)),
            pl.BlockSpec((ka, d), lambda i: (0, 0)),
            pl.BlockSpec((kb, d), lambda i: (0, 0)),
        ],
        out_specs=pl.BlockSpec((tm, d), lambda i: (i, 0)),
        out_shape=jax.ShapeDtypeStruct((t, d), F32),
        compiler_params=_cparams(("parallel",)),
        name="out_proj_residual",
    )(res, a, b, wa, wb)


def _dense_ffn_kernel(h_ref, g_ref, wg_ref, wu_ref, wd_ref, o_ref, xn_ref, acc_ref):
    f = pl.program_id(1)

    @pl.when(f == 0)
    def _():
        xn_ref[...] = _rms(h_ref[...], g_ref[...]).astype(BF16)
        acc_ref[...] = jnp.zeros_like(acc_ref)

    xn = xn_ref[...]
    gate = jnp.dot(xn, wg_ref[...], preferred_element_type=F32)
    up = jnp.dot(xn, wu_ref[...], preferred_element_type=F32)
    act = (gate * jax.nn.sigmoid(gate) * up).astype(BF16)
    acc_ref[...] += jnp.dot(act, wd_ref[...], preferred_element_type=F32)

    @pl.when(f == pl.num_programs(1) - 1)
    def _():
        o_ref[...] = h_ref[...] + acc_ref[...]


def dense_ffn_residual(h2d, g, wg, wu, wd, *, tm, tf):
    t, d = h2d.shape
    fdim = wg.shape[1]
    assert t % tm == 0 and fdim % tf == 0
    return pl.pallas_call(
        _dense_ffn_kernel,
        grid=(t // tm, fdim // tf),
        in_specs=[
            pl.BlockSpec((tm, d), lambda i, f: (i, 0)),
            pl.BlockSpec((1, d), lambda i, f: (0, 0)),
            pl.BlockSpec((d, tf), lambda i, f: (0, f)),
            pl.BlockSpec((d, tf), lambda i, f: (0, f)),
            pl.BlockSpec((tf, d), lambda i, f: (f, 0)),
        ],
        out_specs=pl.BlockSpec((tm, d), lambda i, f: (i, 0)),
        out_shape=jax.ShapeDtypeStruct((t, d), F32),
        scratch_shapes=[pltpu.VMEM((tm, d), BF16), pltpu.VMEM((tm, d), F32)],
        compiler_params=_cparams(("parallel", "arbitrary")),
        name="dense_ffn_residual",
    )(h2d, g.reshape(1, d), wg, wu, wd)


S5_LANE_CHUNK = 512
S5_MXU_TILE = 256


def _s5_kernel(u_ref, bre_ref, bim_ref, cre_ref, cim_ref, stepr_ref, stepi_ref, carr_ref, cari_ref,
               d_ref, gw_ref, gb_ref, o_ref, xr_ref, xi_ref, sr_ref, si_ref):
    s = pl.program_id(1)
    ts = u_ref.shape[1]

    @pl.when(s == 0)
    def _():
        sr_ref[...] = jnp.zeros_like(sr_ref)
        si_ref[...] = jnp.zeros_like(si_ref)

    u = u_ref[0]
    ub = u.astype(BF16)
    ratio = S5_NSTATE // S5_WIDTH
    for n in range(S5_NSTATE // S5_MXU_TILE):
        cols = slice(n * S5_MXU_TILE, (n + 1) * S5_MXU_TILE)
        chans = slice((n // ratio) * S5_MXU_TILE, (n // ratio + 1) * S5_MXU_TILE)
        xr_ref[:, cols] = jnp.dot(ub[:, chans], bre_ref[chans, cols], preferred_element_type=F32)
        xi_ref[:, cols] = jnp.dot(ub[:, chans], bim_ref[chans, cols], preferred_element_type=F32)

    for c in range(S5_NSTATE // S5_LANE_CHUNK):
        lanes = pl.ds(c * S5_LANE_CHUNK, S5_LANE_CHUNK)
        steps = [(stepr_ref[k, :, lanes], stepi_ref[k, :, lanes]) for k in range(3)]
        pr, pi = carr_ref[:, lanes], cari_ref[:, lanes]

        def body(g, carry, lanes=lanes, steps=steps, pr=pr, pi=pi):
            cr, ci = carry
            rows = pl.ds(pl.multiple_of(g * SUBLANES, SUBLANES), SUBLANES)
            xr = xr_ref[rows, lanes]
            xi = xi_ref[rows, lanes]
            for k, (ar, ai) in enumerate(steps):
                rr = pltpu.roll(xr, 1 << k, axis=0)
                ri = pltpu.roll(xi, 1 << k, axis=0)
                xr, xi = xr + ar * rr - ai * ri, xi + ar * ri + ai * rr
            xr = xr + pr * cr - pi * ci
            xi = xi + pr * ci + pi * cr
            xr_ref[rows, lanes] = xr
            xi_ref[rows, lanes] = xi
            return xr[SUBLANES - 1:SUBLANES, :], xi[SUBLANES - 1:SUBLANES, :]

        cr, ci = lax.fori_loop(0, ts // SUBLANES, body, (sr_ref[:, lanes], si_ref[:, lanes]))
        sr_ref[:, lanes] = cr
        si_ref[:, lanes] = ci

    y_tiles = []
    for m in range(S5_WIDTH // S5_MXU_TILE):
        chans = slice(m * S5_MXU_TILE, (m + 1) * S5_MXU_TILE)
        states = slice(m * ratio * S5_MXU_TILE, (m + 1) * ratio * S5_MXU_TILE)
        y_tiles.append(
            jnp.dot(xr_ref[:, states].astype(BF16), cre_ref[states, chans], preferred_element_type=F32)
            - jnp.dot(xi_ref[:, states].astype(BF16), cim_ref[states, chans], preferred_element_type=F32))
    y = jnp.concatenate(y_tiles, axis=1)
    y = jax.nn.gelu(y + d_ref[...] * u)
    z = jnp.dot(y.astype(BF16), gw_ref[...], preferred_element_type=F32) + gb_ref[...]
    o_ref[0] = y * jax.nn.sigmoid(z)


def _s5_params(a_re, a_im, log_dt, b_re, b_im, c_re, c_im):
    dt = jnp.exp(log_dt)[:, None]

    def apow(n):
        mag = jnp.exp(n * dt * a_re)
        return mag * jnp.cos(n * dt * a_im), mag * jnp.sin(n * dt * a_im)

    abar_re, abar_im = apow(1.0)
    den = a_re * a_re + a_im * a_im
    xr, xi = abar_re - 1.0, abar_im
    f_re = (xr * a_re + xi * a_im) / den
    f_im = (xi * a_re - xr * a_im) / den
    bbar_re = f_re[..., None] * b_re - f_im[..., None] * b_im
    bbar_im = f_re[..., None] * b_im + f_im[..., None] * b_re
    eye = jnp.eye(S5_GROUPS, dtype=F32)

    def blockdiag_in(b):
        return jnp.einsum('gpc,gh->gchp', b, eye).reshape(S5_WIDTH, S5_NSTATE)

    def blockdiag_out(c):
        return jnp.einsum('gcp,gh->gphc', c, eye).reshape(S5_NSTATE, S5_WIDTH)

    row = jnp.arange(SUBLANES)[:, None]
    step_r, step_i = [], []
    for k in range(3):
        pr, pi = apow(float(1 << k))
        keep = row >= (1 << k)
        step_r.append(jnp.where(keep, pr.reshape(1, -1), 0.0))
        step_i.append(jnp.where(keep, pi.reshape(1, -1), 0.0))
    car = [apow(float(r + 1)) for r in range(SUBLANES)]
    car_r = jnp.stack([p[0].reshape(-1) for p in car])
    car_i = jnp.stack([p[1].reshape(-1) for p in car])
    return (blockdiag_in(bbar_re).astype(BF16), blockdiag_in(bbar_im).astype(BF16),
            blockdiag_out(c_re).astype(BF16), blockdiag_out(c_im).astype(BF16),
            jnp.stack(step_r), jnp.stack(step_i), car_r, car_i)


def s5_mixer(proj, a_re, a_im, log_dt, b_re, b_im, c_re, c_im, d, glu_w, glu_b, *, ts):
    bsz, seq, _ = proj.shape
    bre, bim, cre, cim, step_r, step_i, car_r, car_i = _s5_params(a_re, a_im, log_dt, b_re, b_im, c_re, c_im)
    full = lambda shape: pl.BlockSpec(shape, lambda b, s: (0,) * len(shape))
    return pl.pallas_call(
        _s5_kernel,
        grid=(bsz, seq // ts),
        in_specs=[
            pl.BlockSpec((1, ts, S5_WIDTH), lambda b, s: (b, s, 0)),
            full((S5_WIDTH, S5_NSTATE)), full((S5_WIDTH, S5_NSTATE)),
            full((S5_NSTATE, S5_WIDTH)), full((S5_NSTATE, S5_WIDTH)),
            full((3, SUBLANES, S5_NSTATE)), full((3, SUBLANES, S5_NSTATE)),
            full((SUBLANES, S5_NSTATE)), full((SUBLANES, S5_NSTATE)),
            full((1, S5_WIDTH)), full((S5_WIDTH, S5_WIDTH)), full((1, S5_WIDTH)),
        ],
        out_specs=pl.BlockSpec((1, ts, S5_WIDTH), lambda b, s: (b, s, 0)),
        out_shape=jax.ShapeDtypeStruct((bsz, seq, S5_WIDTH), F32),
        scratch_shapes=[pltpu.VMEM((ts, S5_NSTATE), F32), pltpu.VMEM((ts, S5_NSTATE), F32),
                        pltpu.VMEM((1, S5_NSTATE), F32), pltpu.VMEM((1, S5_NSTATE), F32)],
        compiler_params=_cparams(("parallel", "arbitrary")),
        name="s5_mixer",
    )(proj, bre, bim, cre, cim, step_r, step_i, car_r, car_i,
      d.reshape(1, -1), glu_w.astype(BF16), glu_b.reshape(1, -1))


_NT = (((1,), (1,)), ((), ()))
_NN = (((1,), (0,)), ((), ()))
_TN = (((0,), (0,)), ((), ()))


def _split(x):
    hi = x.astype(BF16)
    return hi, (x - hi.astype(F32)).astype(BF16)


def _mm(a, b, dims=_NN):
    return lax.dot_general(a.astype(BF16), b.astype(BF16), dims, preferred_element_type=F32)


def _mm3(a, b, dims=_NN):
    ah, al = _split(a)
    bh, bl = _split(b)
    out = lax.dot_general(ah, bh, dims, preferred_element_type=F32)
    out += lax.dot_general(ah, bl, dims, preferred_element_type=F32)
    out += lax.dot_general(al, bh, dims, preferred_element_type=F32)
    return out


def _mm_exact_rhs(a, b01, dims=_NN):
    ah, al = _split(a)
    out = lax.dot_general(ah, b01, dims, preferred_element_type=F32)
    out += lax.dot_general(al, b01, dims, preferred_element_type=F32)
    return out


def _moba_kernel(q_ref, k_ref, v_ref, o_ref, kmean_ref, kb_ref, vt_ref, bias_ref):
    i = pl.program_id(2)
    blk = MOBA_BLOCK
    seq = k_ref.shape[1]
    nb = seq // blk
    nh = LANES // HEAD_DIM

    @pl.when(i == 0)
    def _():
        k = k_ref[0]
        kmean_ref[...] = jnp.mean(k.reshape(nb, blk, LANES), axis=1)
        kb_ref[...] = k.astype(BF16)
        vt_ref[...] = v_ref[0].T.astype(BF16)

    q = q_ref[0]
    lane = lax.broadcasted_iota(jnp.int32, (blk, LANES), 1)
    brow = lax.broadcasted_iota(jnp.int32, (nb, blk), 0)
    krow = lax.broadcasted_iota(jnp.int32, (blk, blk), 0)
    qcol = lax.broadcasted_iota(jnp.int32, (blk, blk), 1)
    own = pl.ds(pl.multiple_of(i * blk, blk), blk)
    k_own, vt_own = kb_ref[own, :], vt_ref[:, own]
    qbs, state = [], []
    for hh in range(nh):
        qh = jnp.where(lane // HEAD_DIM == hh, q, 0.0)
        work = jnp.where(brow < i, _mm3(kmean_ref[...], qh, _NT), -jnp.inf)
        sel = jnp.zeros((nb, blk), F32)
        for _ in range(MOBA_TOPK):
            m = jnp.max(work, axis=0, keepdims=True)
            first = jnp.min(jnp.where(work == m, brow, nb), axis=0, keepdims=True)
            pick = (brow == first) & (m > -jnp.inf)
            sel = jnp.where(pick, 1.0, sel)
            work = jnp.where(pick, -jnp.inf, work)
        bias_ref[hh] = jnp.where(sel > 0.0, 0.0, NEG_BIG)

        qb = (qh * (HEAD_DIM ** -0.5 * LOG2_E)).T.astype(BF16)
        s = jnp.dot(k_own, qb, preferred_element_type=F32)
        s = jnp.where(krow <= qcol, s, NEG_BIG)
        m0 = jnp.max(s, axis=0, keepdims=True)
        p = jnp.exp2(s - m0)
        acc0 = jnp.dot(vt_own, p.astype(BF16), preferred_element_type=F32)
        qbs.append(qb)
        state += [m0, jnp.sum(p, axis=0, keepdims=True), acc0]

    grp, ns = MOBA_BLOCKS_PER_STEP, MOBA_STREAMS
    sub = grp // ns
    for _ in range(1, ns):
        for hh in range(nh):
            state += [jnp.full((1, blk), NEG_BIG, F32), jnp.zeros((1, blk), F32), jnp.zeros((LANES, blk), F32)]

    def body(t, carry):
        out = []
        for sidx in range(ns):
            first = t * grp + sidx * sub
            rows = pl.ds(pl.multiple_of(first * blk, sub * blk), sub * blk)
            kj, vtj = kb_ref[rows, :], vt_ref[:, rows]
            for hh in range(nh):
                m_prev, l_prev, acc = carry[3 * (sidx * nh + hh):3 * (sidx * nh + hh) + 3]
                s = jnp.dot(kj, qbs[hh], preferred_element_type=F32)
                s = jnp.concatenate([s[g * blk:(g + 1) * blk] + bias_ref[hh, pl.ds(first + g, 1), :]
                                     for g in range(sub)], axis=0)
                m_new = jnp.maximum(m_prev, jnp.max(s, axis=0, keepdims=True))
                alpha = jnp.exp2(m_prev - m_new)
                p = jnp.exp2(s - m_new)
                out += [m_new, alpha * l_prev + jnp.sum(p, axis=0, keepdims=True),
                        alpha * acc + jnp.dot(vtj, p.astype(BF16), preferred_element_type=F32)]
        return tuple(out)

    fin = lax.fori_loop(0, (i + grp - 1) // grp, body, tuple(state))
    halves = []
    for hh in range(nh):
        parts = [fin[3 * (sidx * nh + hh):3 * (sidx * nh + hh) + 3] for sidx in range(ns)]
        m_fin = parts[0][0]
        for m_s, _, _ in parts[1:]:
            m_fin = jnp.maximum(m_fin, m_s)
        l_fin = sum(jnp.exp2(m_s - m_fin) * l_s for m_s, l_s, _ in parts)
        acc = sum(jnp.exp2(m_s - m_fin) * a_s for m_s, _, a_s in parts)
        halves.append(acc[hh * HEAD_DIM:(hh + 1) * HEAD_DIM] / l_fin)
    o_ref[0] = jnp.concatenate(halves, axis=0).T


def moba_attention(proj, *, q_col, k_col, v_col):
    bsz, seq, _ = proj.shape
    assert seq % (MOBA_BLOCK * MOBA_BLOCKS_PER_STEP) == 0
    nb = seq // MOBA_BLOCK
    qb, kb, vb = q_col // LANES, k_col // LANES, v_col // LANES
    return pl.pallas_call(
        _moba_kernel,
        grid=(bsz, MOBA_WIDTH // LANES, nb),
        in_specs=[
            pl.BlockSpec((1, MOBA_BLOCK, LANES), lambda b, p, i: (b, i, qb + p)),
            pl.BlockSpec((1, seq, LANES), lambda b, p, i: (b, 0, kb + p)),
            pl.BlockSpec((1, seq, LANES), lambda b, p, i: (b, 0, vb + p)),
        ],
        out_specs=pl.BlockSpec((1, MOBA_BLOCK, LANES), lambda b, p, i: (b, i, p)),
        out_shape=jax.ShapeDtypeStruct((bsz, seq, MOBA_WIDTH), F32),
        scratch_shapes=[pltpu.VMEM((nb, LANES), F32), pltpu.VMEM((seq, LANES), BF16),
                        pltpu.VMEM((LANES, seq), BF16),
                        pltpu.VMEM((LANES // HEAD_DIM, nb, MOBA_BLOCK), F32)],
        compiler_params=_cparams(("parallel", "parallel", "arbitrary")),
        name="moba_attention",
    )(proj, proj, proj)


def _head_sum_matrix(width):
    idx = np.arange(width) // HEAD_DIM
    return jnp.asarray((idx[:, None] == idx[None, :]).astype(np.float32)).astype(BF16)


def _rwkv_prep_kernel(p_ref, prev_ref, mu_ref, w0_ref, wup_ref, a0_ref, aup_ref, gup_ref, kk_ref, ka_ref,
                      rk_ref, hsum_ref, r_out, lw_out, k_out, v_out, kk_out, b_out, g_out, bonus_out):
    s = pl.program_id(1)
    p = p_ref[0]
    ts = p.shape[0]
    row = lax.broadcasted_iota(jnp.int32, p.shape, 0)
    last = jnp.where(s == 0, 0.0, 1.0) * prev_ref[0, SUBLANES - 1:SUBLANES, :]
    prev = jnp.where(row == 0, last, pltpu.roll(p, 1, axis=0))
    p = p + (prev - p) * mu_ref[...]
    w = RWKV_WIDTH
    r, k, v = p[:, 0:w], p[:, w:2 * w], p[:, 2 * w:3 * w]
    lora = p[:, 3 * w:3 * w + LANES]
    gd = p[:, 3 * w + LANES:3 * w + 2 * LANES]
    w_log = -jax.nn.softplus(-(w0_ref[...] + _mm3(jnp.tanh(lora), wup_ref[...]))) - 0.5
    lw_out[0] = -jnp.exp(w_log)
    a = jax.nn.sigmoid(a0_ref[...] + _mm3(lora, aup_ref[...]))
    g_out[0] = _mm(jax.nn.sigmoid(gd), gup_ref[...])
    kk = k * kk_ref[...]
    norm = jnp.sqrt(_mm_exact_rhs(kk * kk, hsum_ref[...]))
    kk = kk / jnp.maximum(norm, 1e-12)
    k = k * (1.0 + (a - 1.0) * ka_ref[...])
    r_out[0] = r
    k_out[0] = k
    v_out[0] = v
    kk_out[0] = kk
    b_out[0] = kk * a
    bonus_out[0] = _mm_exact_rhs(r * k * rk_ref[...], hsum_ref[...]) * v


RWKV_GROUP = 256


def _stack_heads(x):
    lane = lax.broadcasted_iota(jnp.int32, x.shape, 1)
    return jnp.concatenate([jnp.where(lane // HEAD_DIM == hh, x, 0.0)
                            for hh in range(RWKV_GROUP // HEAD_DIM)], axis=0)


def _rwkv_scan_kernel(r_ref, lw_ref, k_ref, v_ref, kk_ref, b_ref, g_ref, bonus_ref, lnw_ref, lnb_ref,
                      havg_ref, o_ref, y_ref, st_ref):
    s = pl.program_id(1)
    ts = r_ref.shape[1]
    cl_ = RWKV_CHUNK
    n = RWKV_GROUP
    ngroups = RWKV_WIDTH // RWKV_GROUP

    @pl.when(s == 0)
    def _():
        st_ref[...] = jnp.zeros_like(st_ref)

    ri = lax.broadcasted_iota(jnp.int32, (n, n), 0)
    ci = lax.broadcasted_iota(jnp.int32, (n, n), 1)
    same = (ri // cl_) == (ci // cl_)
    strict = same & ((ci % cl_) < (ri % cl_))
    incl = same & ((ci % cl_) <= (ri % cl_))
    eye = ri == ci
    tr = lax.broadcasted_iota(jnp.int32, (cl_, cl_), 0)
    tc = lax.broadcasted_iota(jnp.int32, (cl_, cl_), 1)
    cum = jnp.where(tc <= tr, 1.0, 0.0).astype(BF16)

    def chunk(c, _):
        rows = pl.ds(pl.multiple_of(c * cl_, cl_), cl_)
        lw = lw_ref[0, rows, :]
        lw_hi, lw_lo = _split(lw)
        cl_in = (jnp.dot(cum, lw_hi, preferred_element_type=F32)
                 + jnp.dot(cum, lw_lo, preferred_element_type=F32))
        cl_ex = cl_in - lw
        tot = cl_in[cl_ - 1:cl_, :]
        e_in = jnp.exp(cl_in)
        e_ninv = jnp.exp(-cl_in)
        e_rem = jnp.exp(tot - cl_in)
        kk = kk_ref[0, rows, :]
        bb = b_ref[0, rows, :]
        k2 = k_ref[0, rows, :]
        a_t = -kk * jnp.exp(cl_ex)
        b_t = bb * e_ninv
        k_t = k2 * e_ninv
        r_t = r_ref[0, rows, :] * e_in
        b_g = bb * e_rem
        k_g = k2 * e_rem
        vv = v_ref[0, rows, :]
        gam = jnp.exp(tot)
        for grp in range(ngroups):
            ls = slice(grp * n, (grp + 1) * n)
            a_s, b_s, k_s, r_s = (_stack_heads(t[:, ls]) for t in (a_t, b_t, k_t, r_t))
            v_s, bg_s, kg_s = (_stack_heads(t[:, ls]) for t in (vv, b_g, k_g))
            gram = _mm(jnp.concatenate([a_s, r_s], axis=0), jnp.concatenate([b_s, k_s], axis=0), _NT)
            nmat = jnp.where(strict, gram[:n, :n], 0.0)
            m_ak = jnp.where(strict, gram[:n, n:], 0.0)
            m_rb = jnp.where(incl, gram[n:, :n], 0.0)
            m_rk = jnp.where(incl, gram[n:, n:], 0.0)
            winv = jnp.where(eye, 1.0, 0.0) + nmat
            pw = nmat
            for _ in range(int(math.log2(cl_)) - 1):
                pw = _mm(pw, pw)
                winv = winv + _mm(winv, pw)
            st = st_ref[grp]
            ar_st = _mm(jnp.concatenate([a_s, r_s], axis=0), st)
            u_s = _mm(winv, ar_st[:n] + _mm(m_ak, v_s))
            y_s = ar_st[n:] + _mm(m_rb, u_s) + _mm(m_rk, v_s)
            y4 = y_s[0:cl_]
            for hh in range(1, n // cl_):
                y4 = y4 + y_s[hh * cl_:(hh + 1) * cl_]
            y_ref[rows, ls] = y4
            colg = jnp.sum(jnp.where(eye, jnp.broadcast_to(gam[:, ls], (n, n)), 0.0), axis=1, keepdims=True)
            st_ref[grp] = colg * st + _mm(jnp.concatenate([bg_s, kg_s], axis=0),
                                          jnp.concatenate([u_s, v_s], axis=0), _TN)
        return 0

    lax.fori_loop(0, ts // cl_, chunk, 0)

    y = y_ref[...]
    mean = _mm_exact_rhs(y, havg_ref[...])
    yc = y - mean
    var = _mm_exact_rhs(yc * yc, havg_ref[...])
    y = yc * lax.rsqrt(var + RWKV_GN_EPS) * lnw_ref[...] + lnb_ref[...]
    o_ref[0] = (y + bonus_ref[0]) * g_ref[0]


def rwkv7_time_mix(proj, mu, w0, w_up, a0, a_up, g_up, k_k, k_a, r_k, ln_w, ln_b, *, ts):
    bsz, seq, _ = proj.shape
    w = RWKV_WIDTH
    pw = 3 * w + 2 * LANES
    assert RWKV_W_LORA + RWKV_A_LORA == LANES and RWKV_G_LORA == LANES and seq % ts == 0
    wup = jnp.concatenate([w_up, jnp.zeros((RWKV_A_LORA, w), F32)], axis=0)
    aup = jnp.concatenate([jnp.zeros((RWKV_W_LORA, w), F32), a_up], axis=0)
    hsum = _head_sum_matrix(w)
    row = lambda t: t.reshape(1, -1)
    full = lambda shape: pl.BlockSpec(shape, lambda b, s: (0,) * len(shape))
    tile = pl.BlockSpec((1, ts, w), lambda b, s: (b, s, 0))
    nprev = ts // SUBLANES
    outs = pl.pallas_call(
        _rwkv_prep_kernel,
        grid=(bsz, seq // ts),
        in_specs=[
            pl.BlockSpec((1, ts, pw), lambda b, s: (b, s, 0)),
            pl.BlockSpec((1, SUBLANES, pw), lambda b, s: (b, jnp.maximum(s * nprev - 1, 0), 0)),
            full((1, pw)), full((1, w)), full((LANES, w)), full((1, w)), full((LANES, w)),
            full((LANES, w)), full((1, w)), full((1, w)), full((1, w)), full((w, w)),
        ],
        out_specs=[tile] * 8,
        out_shape=[jax.ShapeDtypeStruct((bsz, seq, w), F32)] * 8,
        compiler_params=_cparams(("parallel", "parallel")),
        name="rwkv_prep",
    )(proj, proj, row(mu), row(w0), wup, row(a0), aup, g_up.astype(BF16), row(k_k), row(k_a),
      row(r_k), hsum)
    havg = (_head_sum_matrix(w).astype(F32) / HEAD_DIM).astype(BF16)
    return pl.pallas_call(
        _rwkv_scan_kernel,
        grid=(bsz, seq // ts),
        in_specs=[tile] * 8 + [full((1, w)), full((1, w)), full((w, w))],
        out_specs=tile,
        out_shape=jax.ShapeDtypeStruct((bsz, seq, w), F32),
        scratch_shapes=[pltpu.VMEM((ts, w), F32),
                        pltpu.VMEM((w // RWKV_GROUP, RWKV_GROUP, RWKV_GROUP), F32)],
        compiler_params=_cparams(("parallel", "arbitrary")),
        name="rwkv_scan",
    )(*outs, row(ln_w), row(ln_b), havg)


DSA_KV_TILE = 256
DSA_SEARCH_TILE = 512
DSA_COUNT_ROWS = 64
DSA_STREAMS = 2
DSA_VROWS = 128
INT32_MIN = -2 ** 31
NEG_INF_KEY = -2139095041


def _sortable_key(x, pos, npos):
    bits = pltpu.bitcast(x, jnp.int32)
    key = jnp.where(bits < 0, bits ^ jnp.int32(0x7FFFFFFF), bits + npos)
    return jnp.where(x == 0.0, npos - 1 - pos, key)


def _dsa_kernel(q_ref, qi_ref, wi_ref, k_ref, v_ref, ki_ref, o_ref,
                kb_ref, vt_ref, kib_ref, key_ref, acc_ref):
    i = pl.program_id(1)
    nq, kt, st = DSA_QBLOCK, DSA_KV_TILE, DSA_SEARCH_TILE
    seq = k_ref.shape[1]
    topk = min(DSA_TOPK, seq // 4)

    @pl.when(i == 0)
    def _():
        kb_ref[...] = k_ref[0].astype(BF16)
        vrow = lax.broadcasted_iota(jnp.int32, (DSA_VROWS, seq), 0)
        ones_row = jnp.where(vrow == HEAD_DIM, 1.0, 0.0)
        vt_ref[...] = jnp.where(vrow < HEAD_DIM, v_ref[0].T[:DSA_VROWS], ones_row).astype(BF16)
        kib_ref[...] = ki_ref[0].astype(BF16)
        key_ref[...] = jnp.full(key_ref.shape, NEG_INF_KEY, jnp.int32)

    n_search = (i * nq) // st + 1
    spos = i * nq + lax.broadcasted_iota(jnp.int32, (st, nq), 1)
    lane = lax.broadcasted_iota(jnp.int32, (nq, LANES), 1)
    qpos = i * nq + lax.broadcasted_iota(jnp.int32, (kt, nq), 1)
    krow = lax.broadcasted_iota(jnp.int32, (kt, nq), 0)
    srow = lax.broadcasted_iota(jnp.int32, (st, nq), 0)

    def stack_heads(ref, nheads):
        parts = []
        for h in range(nheads):
            pair = ref[0, :, (h // 2) * LANES:(h // 2 + 1) * LANES]
            parts.append(jnp.where(lane // HEAD_DIM == h % 2, pair, 0.0))
        return jnp.concatenate(parts, axis=0)

    qib_t = stack_heads(qi_ref, DSA_IDX_HEADS).T.astype(BF16)
    wi_t = (wi_ref[0] * (DSA_IDX_HEADS ** -0.5 * HEAD_DIM ** -0.5)).T
    w_lanes = jnp.concatenate([wi_t[h:h + 1, :] for h in range(DSA_IDX_HEADS)], axis=1)

    def score_tile(c, _):
        rows = pl.ds(pl.multiple_of(c * kt, kt), kt)
        logit = jnp.dot(kib_ref[rows, :], qib_t, preferred_element_type=F32)
        weighted = jnp.maximum(logit, 0.0) * w_lanes
        score = weighted[:, 0:nq]
        for h in range(1, DSA_IDX_HEADS):
            score = score + weighted[:, h * nq:(h + 1) * nq]
        kpos = c * kt + krow
        score = jnp.where(kpos <= qpos, score, -jnp.inf)
        key_ref[rows, :] = _sortable_key(score, kpos, seq)
        return 0

    lax.fori_loop(0, n_search * (st // kt), score_tile, 0)

    def count(pred):
        def tile(c, acc):
            rows = pl.ds(pl.multiple_of(c * st, st), st)
            hit = jnp.where(pred(key_ref[rows, :], c * st + srow), 1.0, 0.0)
            for part in range(st // DSA_COUNT_ROWS):
                acc = acc + hit[part * DSA_COUNT_ROWS:(part + 1) * DSA_COUNT_ROWS]
            return acc
        acc = lax.fori_loop(0, n_search, tile, jnp.zeros((DSA_COUNT_ROWS, nq), F32))
        return jnp.sum(acc, axis=0, keepdims=True)

    def search_step(t, base):
        cand = base + lax.shift_left(jnp.int32(1), 31 - t)
        cnt = count(lambda key, pos: key >= cand)
        return jnp.where(cnt >= topk, cand, base)

    thr = lax.fori_loop(0, 32, search_step, jnp.full((1, nq), INT32_MIN, jnp.int32))
    need = topk - count(lambda key, pos: key > thr)
    n_eq = count(lambda key, pos: key == thr)
    tied = jnp.max(jnp.where((n_eq > need) & (thr > NEG_INF_KEY), 1.0, 0.0)) > 0.0
    nbits = seq.bit_length()

    def tie_search(_):
        def step(t, pos):
            cand = pos + lax.shift_left(jnp.int32(1), nbits - 1 - t)
            cnt = count(lambda key, kpos: (key == thr) & (kpos < cand))
            return jnp.where(cnt < need, cand, pos)
        return lax.fori_loop(0, nbits, step, jnp.zeros((1, nq), jnp.int32))

    last = lax.cond(tied, tie_search, lambda _: jnp.full((1, nq), seq, jnp.int32), 0)

    qs_t = (stack_heads(q_ref, DSA_HEADS) * (HEAD_DIM ** -0.5 * LOG2_E)).T.astype(BF16)
    acc_ref[...] = jnp.zeros_like(acc_ref)

    ns = DSA_STREAMS

    def attn_step(t, m_all):
        m_rows = []
        for sidx in range(ns):
            c = t * ns + sidx
            rows = pl.ds(pl.multiple_of(c * st, st), st)
            key = key_ref[rows, :]
            kpos = c * st + srow
            chosen = ((key > thr) | ((key == thr) & (kpos <= last))) & (kpos <= spos)
            bias = jnp.where(chosen, 0.0, NEG_BIG)
            kb, vt = kb_ref[rows, :], vt_ref[:, rows]
            for h in range(DSA_HEADS):
                slot = sidx * DSA_HEADS + h
                s = jnp.dot(kb, qs_t[:, h * nq:(h + 1) * nq], preferred_element_type=F32) + bias
                m_prev = m_all[slot:slot + 1, :]
                m_new = jnp.maximum(m_prev, jnp.max(s, axis=0, keepdims=True))
                p = jnp.exp2(s - m_new).astype(BF16)
                acc_ref[slot] = (jnp.exp2(m_prev - m_new) * acc_ref[slot]
                                 + jnp.dot(vt, p, preferred_element_type=F32))
                m_rows.append(m_new)
        return jnp.concatenate(m_rows, axis=0)

    m_all = lax.fori_loop(0, (n_search + ns - 1) // ns, attn_step,
                          jnp.full((ns * DSA_HEADS, nq), NEG_BIG, F32))

    def head_output(h):
        slots = [sidx * DSA_HEADS + h for sidx in range(ns)]
        m_fin = m_all[slots[0]:slots[0] + 1, :]
        for slot in slots[1:]:
            m_fin = jnp.maximum(m_fin, m_all[slot:slot + 1, :])
        acc = sum(jnp.exp2(m_all[slot:slot + 1, :] - m_fin) * acc_ref[slot] for slot in slots)
        return acc[:HEAD_DIM] / acc[HEAD_DIM:HEAD_DIM + 1]

    for pr in range(DSA_WIDTH // LANES):
        halves = [head_output(2 * pr), head_output(2 * pr + 1)]
        o_ref[0, :, pr * LANES:(pr + 1) * LANES] = jnp.concatenate(halves, axis=0).T


def dsa_attention(proj, *, q_col, qi_col, wi_col, k_col, v_col, ki_col):
    bsz, seq, _ = proj.shape
    assert seq % DSA_SEARCH_TILE == 0 and DSA_SEARCH_TILE % DSA_KV_TILE == 0 and DSA_KV_TILE % DSA_QBLOCK == 0
    nq = seq // DSA_QBLOCK
    kvspec = lambda col: pl.BlockSpec((1, seq, LANES), lambda b, i: (b, 0, col // LANES))
    return pl.pallas_call(
        _dsa_kernel,
        grid=(bsz, nq),
        in_specs=[
            pl.BlockSpec((1, DSA_QBLOCK, DSA_WIDTH), lambda b, i: (b, i, q_col // DSA_WIDTH)),
            pl.BlockSpec((1, DSA_QBLOCK, 2 * LANES), lambda b, i: (b, i, qi_col // (2 * LANES))),
            pl.BlockSpec((1, DSA_QBLOCK, LANES), lambda b, i: (b, i, wi_col // LANES)),
            kvspec(k_col), kvspec(v_col), kvspec(ki_col),
        ],
        out_specs=pl.BlockSpec((1, DSA_QBLOCK, DSA_WIDTH), lambda b, i: (b, i, 0)),
        out_shape=jax.ShapeDtypeStruct((bsz, seq, DSA_WIDTH), F32),
        scratch_shapes=[pltpu.VMEM((seq, LANES), BF16), pltpu.VMEM((DSA_VROWS, seq), BF16),
                        pltpu.VMEM((seq, LANES), BF16),
                        pltpu.VMEM((seq, DSA_QBLOCK), jnp.int32),
                        pltpu.VMEM((DSA_STREAMS * DSA_HEADS, DSA_VROWS, DSA_QBLOCK), F32)],
        compiler_params=_cparams(("parallel", "arbitrary")),
        name="dsa_attention",
    )(proj, proj, proj, proj, proj, proj)


MOE_TOKEN_TILE = 2048
MOE_ROW_BLOCK = 256
MOE_TAIL_BLOCK = 128


def _router_kernel(h_ref, g_ref, wr_ref, xn_ref, route_ref):
    xn = _rms(h_ref[...], g_ref[...])
    xn_ref[...] = xn.astype(BF16)
    lane = lax.broadcasted_iota(jnp.int32, (xn.shape[0], LANES), 1)
    logits = jnp.where(lane < N_EXPERTS, _mm3(xn, wr_ref[...]), -jnp.inf)
    v1 = jnp.max(logits, axis=1, keepdims=True)
    e1 = jnp.min(jnp.where(logits == v1, lane, LANES), axis=1, keepdims=True)
    rest = jnp.where(lane == e1, -jnp.inf, logits)
    v2 = jnp.max(rest, axis=1, keepdims=True)
    e2 = jnp.min(jnp.where(rest == v2, lane, LANES), axis=1, keepdims=True)
    ratio = jnp.exp(v2 - v1)
    g1 = 1.0 / (1.0 + ratio)
    g2 = ratio * g1
    route_ref[...] = jnp.where(lane == 0, e1.astype(F32), jnp.where(
        lane == 1, e2.astype(F32), jnp.where(lane == 2, g1, jnp.where(lane == 3, g2, 0.0))))


def route_tokens(h2d, g, router, *, tm):
    t, d = h2d.shape
    wr = jnp.zeros((d, LANES), F32).at[:, :N_EXPERTS].set(router)
    return pl.pallas_call(
        _router_kernel,
        grid=(t // tm,),
        in_specs=[pl.BlockSpec((tm, d), lambda i: (i, 0)), pl.BlockSpec((1, d), lambda i: (0, 0)),
                  pl.BlockSpec((d, LANES), lambda i: (0, 0))],
        out_specs=[pl.BlockSpec((tm, d), lambda i: (i, 0)), pl.BlockSpec((tm, LANES), lambda i: (i, 0))],
        out_shape=[jax.ShapeDtypeStruct((t, d), BF16), jax.ShapeDtypeStruct((t, LANES), F32)],
        compiler_params=_cparams(("parallel",)),
        name="moe_router",
    )(h2d, g.reshape(1, d), wr)


def _moe_kernel(cnt_ref, xn_ref, rk_ref, gt_ref, rkt_ref, wg_ref, wu_ref, wd_ref, o_ref, xs_ref, acc_ref):
    tau, e, f = pl.program_id(0), pl.program_id(1), pl.program_id(2)
    tt = xn_ref.shape[0]
    big, small = MOE_ROW_BLOCK, MOE_TAIL_BLOCK
    n_small = (cnt_ref[tau * N_EXPERTS + e] + small - 1) // small
    n_big = n_small // (big // small)
    has_tail = n_small % (big // small) == 1

    def for_each_block(step):
        lax.fori_loop(0, n_big, lambda b, _: step(pl.multiple_of(b * big, big), big), 0)

        @pl.when(has_tail)
        def _():
            step(pl.multiple_of(n_big * big, big), small)

    @pl.when((e == 0) & (f == 0))
    def _():
        o_ref[...] = jnp.zeros_like(o_ref)

    @pl.when(f == 0)
    def _():
        rank = rk_ref[0, pl.ds(e, 1), :]

        def gather(first, rows):
            slot = first + lax.broadcasted_iota(jnp.int32, (rows, tt), 0)
            onehot = jnp.where(rank == slot, 1.0, 0.0).astype(BF16)
            xs_ref[pl.ds(first, rows), :] = jnp.dot(onehot, xn_ref[...], preferred_element_type=F32).astype(BF16)
            acc_ref[pl.ds(first, rows), :] = jnp.zeros((rows, acc_ref.shape[1]), F32)
            return 0

        for_each_block(gather)

    def ffn(first, rows):
        xs = xs_ref[pl.ds(first, rows), :]
        gate = jnp.dot(xs, wg_ref[0], preferred_element_type=F32)
        up = jnp.dot(xs, wu_ref[0], preferred_element_type=F32)
        act = (gate * jax.nn.sigmoid(gate) * up).astype(BF16)
        acc_ref[pl.ds(first, rows), :] += jnp.dot(act, wd_ref[0], preferred_element_type=F32)
        return 0

    for_each_block(ffn)

    @pl.when(f == pl.num_programs(2) - 1)
    def _():
        rank = rk_ref[0, pl.ds(e, 1), :]
        gates = gt_ref[0, pl.ds(e, 1), :]
        lane_e = lax.broadcasted_iota(jnp.int32, rkt_ref.shape[1:], 1)
        rank_t = jnp.sum(jnp.where(lane_e == e, rkt_ref[0], 0), axis=1, keepdims=True)

        def combine(first, rows):
            slot = first + lax.broadcasted_iota(jnp.int32, (rows, tt), 0)
            gate_slot = jnp.sum(jnp.where(rank == slot, gates, 0.0), axis=1, keepdims=True)
            yg = (acc_ref[pl.ds(first, rows), :] * gate_slot).astype(BF16)
            slot_t = first + lax.broadcasted_iota(jnp.int32, (tt, rows), 1)
            scatter = jnp.where(rank_t == slot_t, 1.0, 0.0).astype(BF16)
            o_ref[...] += jnp.dot(scatter, yg, preferred_element_type=F32)
            return 0

        for_each_block(combine)


def moe_experts(xn_bf16, route, w_gate, w_up, w_down, *, tf):
    t, d = xn_bf16.shape
    tt = MOE_TOKEN_TILE
    ntile = t // tt
    fdim = w_gate.shape[2]
    experts = route[:, 0:2].astype(jnp.int32).reshape(ntile, tt * 2)
    gates = route[:, 2:4].reshape(ntile, tt * 2)
    onehot = (experts[:, :, None] == jnp.arange(N_EXPERTS)[None, None, :]).astype(jnp.int32)
    csum = jnp.cumsum(onehot, axis=1)
    counts = csum[:, -1, :]
    rank = jnp.where(onehot > 0, csum - 1, -1)
    rank_tok = jnp.max(rank.reshape(ntile, tt, 2, N_EXPERTS), axis=2)
    gate_tok = jnp.sum((onehot * gates[:, :, None]).reshape(ntile, tt, 2, N_EXPERTS), axis=2)
    rk = jnp.swapaxes(rank_tok, 1, 2)
    gt = jnp.swapaxes(gate_tok, 1, 2)
    grid_spec = pltpu.PrefetchScalarGridSpec(
        num_scalar_prefetch=1,
        grid=(ntile, N_EXPERTS, fdim // tf),
        in_specs=[
            pl.BlockSpec((tt, d), lambda i, e, f, c: (i, 0)),
            pl.BlockSpec((1, N_EXPERTS, tt), lambda i, e, f, c: (i, 0, 0)),
            pl.BlockSpec((1, N_EXPERTS, tt), lambda i, e, f, c: (i, 0, 0)),
            pl.BlockSpec((1, tt, N_EXPERTS), lambda i, e, f, c: (i, 0, 0)),
            pl.BlockSpec((1, d, tf), lambda i, e, f, c: (e, 0, f)),
            pl.BlockSpec((1, d, tf), lambda i, e, f, c: (e, 0, f)),
            pl.BlockSpec((1, tf, d), lambda i, e, f, c: (e, f, 0)),
        ],
        out_specs=pl.BlockSpec((tt, d), lambda i, e, f, c: (i, 0)),
        scratch_shapes=[pltpu.VMEM((tt, d), BF16), pltpu.VMEM((tt, d), F32)],
    )
    return pl.pallas_call(
        _moe_kernel,
        grid_spec=grid_spec,
        out_shape=jax.ShapeDtypeStruct((t, d), F32),
        compiler_params=_cparams(("parallel", "arbitrary", "arbitrary")),
        name="moe_experts",
    )(counts.reshape(-1), xn_bf16, rk, gt, rank_tok, w_gate, w_up, w_down)


def _add_norm_kernel(a_ref, b_ref, g_ref, o_ref, *, normalize):
    y = a_ref[...] + b_ref[...]
    o_ref[...] = _rms(y, g_ref[...]) if normalize else y


def add_rmsnorm(a, b, g, *, tm):
    t, d = a.shape
    spec = pl.BlockSpec((tm, d), lambda i: (i, 0))
    gain = jnp.ones((1, d), F32) if g is None else g.reshape(1, d)
    return pl.pallas_call(
        functools.partial(_add_norm_kernel, normalize=g is not None),
        grid=(t // tm,),
        in_specs=[spec, spec, pl.BlockSpec((1, d), lambda i: (0, 0))],
        out_specs=spec,
        out_shape=jax.ShapeDtypeStruct((t, d), F32),
        compiler_params=_cparams(("parallel",)),
        name="add_rmsnorm",
    )(a, b, gain)


ROW_TILE = 512
FFN_ROW_TILE = 1024
FFN_COL_TILE = 256

EVEN_Q_COL, EVEN_K_COL, EVEN_V_COL = 512, 1024, 1536
EVEN_ROPE_COLS = np.zeros(2048, bool)
EVEN_ROPE_COLS[EVEN_Q_COL:EVEN_V_COL] = True


def _odd_layout():
    rw = 3 * RWKV_WIDTH + RWKV_W_LORA + RWKV_A_LORA + RWKV_G_LORA
    q0 = rw
    k0 = q0 + DSA_WIDTH
    v0 = k0 + HEAD_DIM
    qi0 = v0 + HEAD_DIM
    ki0 = qi0 + DSA_IDX_HEADS * HEAD_DIM
    wi0 = ki0 + HEAD_DIM
    seg = lambda start, n: list(range(start, start + n))
    src, keep, rope = [], [], []

    def add(cols, roped, pad=0):
        src.extend(cols + [0] * pad)
        keep.extend([1.0] * len(cols) + [0.0] * pad)
        rope.extend([roped] * (len(cols) + pad))

    add(seg(0, rw), False)
    cols = {"k_col": len(src)}
    add(seg(k0, HEAD_DIM) * 2, True)
    cols["v_col"] = len(src)
    add(seg(v0, HEAD_DIM) * 2, False)
    cols["q_col"] = len(src)
    add(seg(q0, DSA_WIDTH), True)
    cols["qi_col"] = len(src)
    add(seg(qi0, DSA_IDX_HEADS * HEAD_DIM), True)
    cols["ki_col"] = len(src)
    add(seg(ki0, HEAD_DIM) * 2, True)
    cols["wi_col"] = len(src)
    add(seg(wi0, DSA_IDX_HEADS), False, pad=LANES - DSA_IDX_HEADS)
    return (np.asarray(src, np.int32), np.asarray(keep, np.float32), np.asarray(rope, bool), cols)


ODD_COL_SRC, ODD_COL_KEEP, ODD_ROPE_COLS, ODD_DSA_COLS = _odd_layout()


def kernel(x, e_norm_mix, e_w_in, s5_a_re, s5_a_im, s5_log_dt, s5_b_re, s5_b_im, s5_c_re, s5_c_im, s5_d, s5_glu_w, s5_glu_b, e_w_out, e_norm_ffn, ffn_w_gate, ffn_w_up, ffn_w_down, o_norm_mix, o_w_in, rwkv_mu, rwkv_w0, rwkv_w_up, rwkv_a0, rwkv_a_up, rwkv_g_up, rwkv_k_k, rwkv_k_a, rwkv_r_k, rwkv_ln_w, rwkv_ln_b, o_w_out, o_norm_ffn, moe_router, moe_w_gate, moe_w_up, moe_w_down, final_norm):
    bsz, seq, d = x.shape
    n_even, n_odd = e_norm_mix.shape[0], o_norm_mix.shape[0]
    h = x.reshape(-1, d)
    zero = jnp.zeros((bsz * seq, d), F32)
    for layer in range(n_even + n_odd):
        j = layer // 2
        last = layer == n_even + n_odd - 1
        if layer % 2 == 0:
            proj = norm_matmul_rope(h, e_norm_mix[j], e_w_in[j].astype(BF16), EVEN_ROPE_COLS, seq,
                                    tm=ROW_TILE, tn=512).reshape(bsz, seq, -1)
            ya = s5_mixer(proj, s5_a_re[j], s5_a_im[j], s5_log_dt[j], s5_b_re[j], s5_b_im[j], s5_c_re[j],
                          s5_c_im[j], s5_d[j], s5_glu_w[j], s5_glu_b[j], ts=256)
            yb = moba_attention(proj, q_col=EVEN_Q_COL, k_col=EVEN_K_COL, v_col=EVEN_V_COL)
            h = out_proj_residual(h, ya.reshape(-1, S5_WIDTH), yb.reshape(-1, MOBA_WIDTH),
                                  e_w_out[j].astype(BF16), tm=ROW_TILE)
            h = dense_ffn_residual(h, e_norm_ffn[j], ffn_w_gate[j].astype(BF16), ffn_w_up[j].astype(BF16),
                                   ffn_w_down[j].astype(BF16), tm=FFN_ROW_TILE, tf=FFN_COL_TILE)
            if last:
                h = add_rmsnorm(h, zero, final_norm, tm=ROW_TILE)
        else:
            w_in = jnp.take(o_w_in[j], jnp.asarray(ODD_COL_SRC), axis=1) * jnp.asarray(ODD_COL_KEEP)[None, :]
            proj = norm_matmul_rope(h, o_norm_mix[j], w_in.astype(BF16), ODD_ROPE_COLS, seq,
                                    tm=ROW_TILE, tn=512).reshape(bsz, seq, -1)
            yc = rwkv7_time_mix(proj, rwkv_mu[j], rwkv_w0[j], rwkv_w_up[j], rwkv_a0[j], rwkv_a_up[j],
                                rwkv_g_up[j], rwkv_k_k[j], rwkv_k_a[j], rwkv_r_k[j], rwkv_ln_w[j],
                                rwkv_ln_b[j], ts=256)
            yd = dsa_attention(proj, **ODD_DSA_COLS)
            h = out_proj_residual(h, yc.reshape(-1, RWKV_WIDTH), yd.reshape(-1, DSA_WIDTH),
                                  o_w_out[j].astype(BF16), tm=ROW_TILE)
            xn, route = route_tokens(h, o_norm_ffn[j], moe_router[j], tm=ROW_TILE)
            y = moe_experts(xn, route, moe_w_gate[j].astype(BF16), moe_w_up[j].astype(BF16),
                            moe_w_down[j].astype(BF16), tf=512)
            h = add_rmsnorm(h, y, final_norm if last else None, tm=ROW_TILE)
    return h.reshape(bsz, seq, d)
```

```python
import functools
import math

import jax
import jax.numpy as jnp
import numpy as np
from jax import lax
from jax.experimental import pallas as pl
from jax.experimental.pallas import tpu as pltpu

F32 = jnp.float32
BF16 = jnp.bfloat16
HIGHEST = lax.Precision.HIGHEST

LANES = 128
SUBLANES = 8
VMEM_LIMIT_BYTES = 56 * 1024 * 1024

D_MODEL = 1024
HEAD_DIM = 64
ROT_DIM = HEAD_DIM // 4
ROPE_THETA = 500000.0
NORM_EPS = 1e-6
S5_WIDTH = 512
S5_GROUP = 16
S5_GROUPS = S5_WIDTH // S5_GROUP
S5_STATE = 64
S5_NSTATE = S5_GROUPS * S5_STATE
MOBA_WIDTH = 512
MOBA_HEADS = MOBA_WIDTH // HEAD_DIM
MOBA_BLOCK = 256
MOBA_TOPK = 3
MOBA_BLOCKS_PER_STEP = 8
MOBA_STREAMS = 1
RWKV_WIDTH = 512
RWKV_HEADS = RWKV_WIDTH // HEAD_DIM
RWKV_W_LORA = 64
RWKV_A_LORA = 64
RWKV_G_LORA = 128
RWKV_GN_EPS = 64e-5
RWKV_CHUNK = 64
DSA_WIDTH = 512
DSA_HEADS = DSA_WIDTH // HEAD_DIM
DSA_IDX_HEADS = 4
DSA_TOPK = 256
DSA_QBLOCK = 128
N_EXPERTS = 8
NEG_BIG = -1e30
LOG2_E = 1.4426950408889634


def _cparams(sem):
    return pltpu.CompilerParams(dimension_semantics=sem, vmem_limit_bytes=VMEM_LIMIT_BYTES)


def _rms(x, g):
    return x * lax.rsqrt(jnp.mean(x * x, axis=-1, keepdims=True) + NORM_EPS) * g


def _norm_matmul_kernel(flags_ref, h_ref, g_ref, w_ref, cos_ref, sina_ref, sinb_ref, colf_ref,
                        o_ref, xn_ref):
    j = pl.program_id(1)

    @pl.when(j == 0)
    def _():
        xn_ref[...] = _rms(h_ref[...], g_ref[...]).astype(BF16)

    acc = jnp.dot(xn_ref[...], w_ref[...], preferred_element_type=F32)
    tn = acc.shape[1]

    @pl.when(flags_ref[j] == 0)
    def _():
        o_ref[...] = acc

    @pl.when(flags_ref[j] != 0)
    def _():
        reps = tn // LANES
        colf = colf_ref[...]
        cos = 1.0 + colf * (jnp.tile(cos_ref[...], (1, reps)) - 1.0)
        sina = colf * jnp.tile(sina_ref[...], (1, reps))
        sinb = colf * jnp.tile(sinb_ref[...], (1, reps))
        nxt = pltpu.roll(acc, tn - ROT_DIM // 2, axis=1)
        prv = pltpu.roll(acc, ROT_DIM // 2, axis=1)
        o_ref[...] = acc * cos + nxt * sina + prv * sinb


def _rope_tables(seq):
    half = ROT_DIM // 2
    inv_freq = ROPE_THETA ** (-jnp.arange(half, dtype=F32) / half)
    ang = jnp.arange(seq, dtype=F32)[:, None] * inv_freq[None, :]
    cos, sin = jnp.cos(ang), jnp.sin(ang)
    ones = jnp.ones((seq, HEAD_DIM - ROT_DIM), F32)
    zeros = jnp.zeros((seq, HEAD_DIM - ROT_DIM), F32)
    zh = jnp.zeros((seq, half), F32)
    cos64 = jnp.concatenate([cos, cos, ones], axis=1)
    sina64 = jnp.concatenate([-sin, zh, zeros], axis=1)
    sinb64 = jnp.concatenate([zh, sin, zeros], axis=1)
    rep = LANES // HEAD_DIM
    return jnp.tile(cos64, (1, rep)), jnp.tile(sina64, (1, rep)), jnp.tile(sinb64, (1, rep))


def norm_matmul_rope(h2d, g, w_bf16, rope_cols, seq, *, tm, tn):
    t, d = h2d.shape
    n = w_bf16.shape[1]
    assert t % tm == 0 and n % tn == 0 and seq % tm == 0 and tn % LANES == 0
    nseq = seq // tm
    cos, sina, sinb = _rope_tables(seq)
    tile_flags = jnp.asarray(rope_cols.reshape(n // tn, tn).any(axis=1).astype(np.int32))
    colf = jnp.asarray(rope_cols.astype(np.float32))[None, :]
    grid_spec = pltpu.PrefetchScalarGridSpec(
        num_scalar_prefetch=1,
        grid=(t // tm, n // tn),
        in_specs=[
            pl.BlockSpec((tm, d), lambda i, j, f: (i, 0)),
            pl.BlockSpec((1, d), lambda i, j, f: (0, 0)),
            pl.BlockSpec((d, tn), lambda i, j, f: (0, j)),
            pl.BlockSpec((tm, LANES), lambda i, j, f: (i % nseq, 0)),
            pl.BlockSpec((tm, LANES), lambda i, j, f: (i % nseq, 0)),
            pl.BlockSpec((tm, LANES), lambda i, j, f: (i % nseq, 0)),
            pl.BlockSpec((1, tn), lambda i, j, f: (0, j)),
        ],
        out_specs=pl.BlockSpec((tm, tn), lambda i, j, f: (i, j)),
        scratch_shapes=[pltpu.VMEM((tm, d), BF16)],
    )
    return pl.pallas_call(
        _norm_matmul_kernel,
        grid_spec=grid_spec,
        out_shape=jax.ShapeDtypeStruct((t, n), F32),
        compiler_params=_cparams(("parallel", "arbitrary")),
        name="norm_matmul_rope",
    )(tile_flags, h2d, g.reshape(1, d), w_bf16, cos, sina, sinb, colf)


def _dense_ffn_kernel(res_ref, a_ref, b_ref, wa_ref, wb_ref, g_ref, wg_ref, wu_ref, wd_ref, o_ref,
                      h_ref, xn_ref, acc_ref):
    f = pl.program_id(1)

    @pl.when(f == 0)
    def _():
        mix = jnp.dot(a_ref[...].astype(BF16), wa_ref[...], preferred_element_type=F32)
        mix += jnp.dot(b_ref[...].astype(BF16), wb_ref[...], preferred_element_type=F32)
        h_ref[...] = res_ref[...] + mix
        xn_ref[...] = _rms(h_ref[...], g_ref[...]).astype(BF16)
        acc_ref[...] = jnp.zeros_like(acc_ref)

    xn = xn_ref[...]
    gate = jnp.dot(xn, wg_ref[...], preferred_element_type=F32)
    up = jnp.dot(xn, wu_ref[...], preferred_element_type=F32)
    act = (gate * jax.nn.sigmoid(gate) * up).astype(BF16)
    acc_ref[...] += jnp.dot(act, wd_ref[...], preferred_element_type=F32)

    @pl.when(f == pl.num_programs(1) - 1)
    def _():
        o_ref[...] = h_ref[...] + acc_ref[...]


def out_proj_dense_ffn(res, a, b, w_out, g, wg, wu, wd, *, tm, tf):
    t, d = res.shape
    ka, kb = a.shape[1], b.shape[1]
    fdim = wg.shape[1]
    assert t % tm == 0 and fdim % tf == 0
    return pl.pallas_call(
        _dense_ffn_kernel,
        grid=(t // tm, fdim // tf),
        in_specs=[
            pl.BlockSpec((tm, d), lambda i, f: (i, 0)),
            pl.BlockSpec((tm, ka), lambda i, f: (i, 0)),
            pl.BlockSpec((tm, kb), lambda i, f: (i, 0)),
            pl.BlockSpec((ka, d), lambda i, f: (0, 0)),
            pl.BlockSpec((kb, d), lambda i, f: (0, 0)),
            pl.BlockSpec((1, d), lambda i, f: (0, 0)),
            pl.BlockSpec((d, tf), lambda i, f: (0, f)),
            pl.BlockSpec((d, tf), lambda i, f: (0, f)),
            pl.BlockSpec((tf, d), lambda i, f: (f, 0)),
        ],
        out_specs=pl.BlockSpec((tm, d), lambda i, f: (i, 0)),
        out_shape=jax.ShapeDtypeStruct((t, d), F32),
        scratch_shapes=[pltpu.VMEM((tm, d), F32), pltpu.VMEM((tm, d), BF16), pltpu.VMEM((tm, d), F32)],
        compiler_params=_cparams(("parallel", "arbitrary")),
        name="out_proj_dense_ffn",
    )(res, a, b, w_out[:ka], w_out[ka:], g.reshape(1, d), wg, wu, wd)


S5_LANE_CHUNK = 512
S5_MXU_TILE = 256


def _s5_kernel(u_ref, bre_ref, bim_ref, cre_ref, cim_ref, stepr_ref, stepi_ref, carr_ref, cari_ref,
               d_ref, gw_ref, gb_ref, o_ref, xr_ref, xi_ref, sr_ref, si_ref):
    s = pl.program_id(1)
    ts = u_ref.shape[1]

    @pl.when(s == 0)
    def _():
        sr_ref[...] = jnp.zeros_like(sr_ref)
        si_ref[...] = jnp.zeros_like(si_ref)

    u = u_ref[0]
    ub = u.astype(BF16)
    ratio = S5_NSTATE // S5_WIDTH
    for n in range(S5_NSTATE // S5_MXU_TILE):
        cols = slice(n * S5_MXU_TILE, (n + 1) * S5_MXU_TILE)
        chans = slice((n // ratio) * S5_MXU_TILE, (n // ratio + 1) * S5_MXU_TILE)
        xr_ref[:, cols] = jnp.dot(ub[:, chans], bre_ref[chans, cols], preferred_element_type=F32)
        xi_ref[:, cols] = jnp.dot(ub[:, chans], bim_ref[chans, cols], preferred_element_type=F32)

    for c in range(S5_NSTATE // S5_LANE_CHUNK):
        lanes = pl.ds(c * S5_LANE_CHUNK, S5_LANE_CHUNK)
        steps = [(stepr_ref[k, :, lanes], stepi_ref[k, :, lanes]) for k in range(3)]
        pr, pi = carr_ref[:, lanes], cari_ref[:, lanes]

        def body(g, carry, lanes=lanes, steps=steps, pr=pr, pi=pi):
            cr, ci = carry
            rows = pl.ds(pl.multiple_of(g * SUBLANES, SUBLANES), SUBLANES)
            xr = xr_ref[rows, lanes]
            xi = xi_ref[rows, lanes]
            for k, (ar, ai) in enumerate(steps):
                rr = pltpu.roll(xr, 1 << k, axis=0)
                ri = pltpu.roll(xi, 1 << k, axis=0)
                xr, xi = xr + ar * rr - ai * ri, xi + ar * ri + ai * rr
            xr = xr + pr * cr - pi * ci
            xi = xi + pr * ci + pi * cr
            xr_ref[rows, lanes] = xr
            xi_ref[rows, lanes] = xi
            return xr[SUBLANES - 1:SUBLANES, :], xi[SUBLANES - 1:SUBLANES, :]

        cr, ci = lax.fori_loop(0, ts // SUBLANES, body, (sr_ref[:, lanes], si_ref[:, lanes]))
        sr_ref[:, lanes] = cr
        si_ref[:, lanes] = ci

    y_tiles = []
    for m in range(S5_WIDTH // S5_MXU_TILE):
        chans = slice(m * S5_MXU_TILE, (m + 1) * S5_MXU_TILE)
        states = slice(m * ratio * S5_MXU_TILE, (m + 1) * ratio * S5_MXU_TILE)
        y_tiles.append(
            jnp.dot(xr_ref[:, states].astype(BF16), cre_ref[states, chans], preferred_element_type=F32)
            - jnp.dot(xi_ref[:, states].astype(BF16), cim_ref[states, chans], preferred_element_type=F32))
    y = jnp.concatenate(y_tiles, axis=1)
    y = jax.nn.gelu(y + d_ref[...] * u)
    z = jnp.dot(y.astype(BF16), gw_ref[...], preferred_element_type=F32) + gb_ref[...]
    o_ref[0] = y * jax.nn.sigmoid(z)


def _s5_params(a_re, a_im, log_dt, b_re, b_im, c_re, c_im):
    dt = jnp.exp(log_dt)[:, None]

    def apow(n):
        mag = jnp.exp(n * dt * a_re)
        return mag * jnp.cos(n * dt * a_im), mag * jnp.sin(n * dt * a_im)

    abar_re, abar_im = apow(1.0)
    den = a_re * a_re + a_im * a_im
    xr, xi = abar_re - 1.0, abar_im
    f_re = (xr * a_re + xi * a_im) / den
    f_im = (xi * a_re - xr * a_im) / den
    bbar_re = f_re[..., None] * b_re - f_im[..., None] * b_im
    bbar_im = f_re[..., None] * b_im + f_im[..., None] * b_re
    eye = jnp.eye(S5_GROUPS, dtype=F32)

    def blockdiag_in(b):
        return jnp.einsum('gpc,gh->gchp', b, eye).reshape(S5_WIDTH, S5_NSTATE)

    def blockdiag_out(c):
        return jnp.einsum('gcp,gh->gphc', c, eye).reshape(S5_NSTATE, S5_WIDTH)

    row = jnp.arange(SUBLANES)[:, None]
    step_r, step_i = [], []
    for k in range(3):
        pr, pi = apow(float(1 << k))
        keep = row >= (1 << k)
        step_r.append(jnp.where(keep, pr.reshape(1, -1), 0.0))
        step_i.append(jnp.where(keep, pi.reshape(1, -1), 0.0))
    car = [apow(float(r + 1)) for r in range(SUBLANES)]
    car_r = jnp.stack([p[0].reshape(-1) for p in car])
    car_i = jnp.stack([p[1].reshape(-1) for p in car])
    return (blockdiag_in(bbar_re).astype(BF16), blockdiag_in(bbar_im).astype(BF16),
            blockdiag_out(c_re).astype(BF16), blockdiag_out(c_im).astype(BF16),
            jnp.stack(step_r), jnp.stack(step_i), car_r, car_i)


def s5_mixer(proj, a_re, a_im, log_dt, b_re, b_im, c_re, c_im, d, glu_w, glu_b, *, ts):
    bsz, seq, _ = proj.shape
    bre, bim, cre, cim, step_r, step_i, car_r, car_i = _s5_params(a_re, a_im, log_dt, b_re, b_im, c_re, c_im)
    full = lambda shape: pl.BlockSpec(shape, lambda b, s: (0,) * len(shape))
    return pl.pallas_call(
        _s5_kernel,
        grid=(bsz, seq // ts),
        in_specs=[
            pl.BlockSpec((1, ts, S5_WIDTH), lambda b, s: (b, s, 0)),
            full((S5_WIDTH, S5_NSTATE)), full((S5_WIDTH, S5_NSTATE)),
            full((S5_NSTATE, S5_WIDTH)), full((S5_NSTATE, S5_WIDTH)),
            full((3, SUBLANES, S5_NSTATE)), full((3, SUBLANES, S5_NSTATE)),
            full((SUBLANES, S5_NSTATE)), full((SUBLANES, S5_NSTATE)),
            full((1, S5_WIDTH)), full((S5_WIDTH, S5_WIDTH)), full((1, S5_WIDTH)),
        ],
        out_specs=pl.BlockSpec((1, ts, S5_WIDTH), lambda b, s: (b, s, 0)),
        out_shape=jax.ShapeDtypeStruct((bsz, seq, S5_WIDTH), F32),
        scratch_shapes=[pltpu.VMEM((ts, S5_NSTATE), F32), pltpu.VMEM((ts, S5_NSTATE), F32),
                        pltpu.VMEM((1, S5_NSTATE), F32), pltpu.VMEM((1, S5_NSTATE), F32)],
        compiler_params=_cparams(("parallel", "arbitrary")),
        name="s5_mixer",
    )(proj, bre, bim, cre, cim, step_r, step_i, car_r, car_i,
      d.reshape(1, -1), glu_w.astype(BF16), glu_b.reshape(1, -1))


_NT = (((1,), (1,)), ((), ()))
_NN = (((1,), (0,)), ((), ()))
_TN = (((0,), (0,)), ((), ()))


def _split(x):
    hi = x.astype(BF16)
    return hi, (x - hi.astype(F32)).astype(BF16)


def _mm(a, b, dims=_NN):
    return lax.dot_general(a.astype(BF16), b.astype(BF16), dims, preferred_element_type=F32)


def _mm3(a, b, dims=_NN):
    ah, al = _split(a)
    bh, bl = _split(b)
    out = lax.dot_general(ah, bh, dims, preferred_element_type=F32)
    out += lax.dot_general(ah, bl, dims, preferred_element_type=F32)
    out += lax.dot_general(al, bh, dims, preferred_element_type=F32)
    return out


def _mm_exact_rhs(a, b01, dims=_NN):
    ah, al = _split(a)
    out = lax.dot_general(ah, b01, dims, preferred_element_type=F32)
    out += lax.dot_general(al, b01, dims, preferred_element_type=F32)
    return out


def _moba_kernel(q_ref, k_ref, v_ref, o_ref, kmean_ref, kb_ref, vt_ref, bias_ref):
    i = pl.program_id(2)
    blk = MOBA_BLOCK
    seq = k_ref.shape[1]
    nb = seq // blk
    nh = LANES // HEAD_DIM

    @pl.when(i == 0)
    def _():
        k = k_ref[0]
        kmean_ref[...] = jnp.mean(k.reshape(nb, blk, LANES), axis=1)
        kb_ref[...] = k.astype(BF16)
        vt_ref[...] = v_ref[0].T.astype(BF16)

    q = q_ref[0]
    lane = lax.broadcasted_iota(jnp.int32, (blk, LANES), 1)
    brow = lax.broadcasted_iota(jnp.int32, (nb, blk), 0)
    krow = lax.broadcasted_iota(jnp.int32, (blk, blk), 0)
    qcol = lax.broadcasted_iota(jnp.int32, (blk, blk), 1)
    own = pl.ds(pl.multiple_of(i * blk, blk), blk)
    k_own, vt_own = kb_ref[own, :], vt_ref[:, own]
    qbs, state = [], []
    for hh in range(nh):
        qh = jnp.where(lane // HEAD_DIM == hh, q, 0.0)
        work = jnp.where(brow < i, _mm3(kmean_ref[...], qh, _NT), -jnp.inf)
        sel = jnp.zeros((nb, blk), F32)
        for _ in range(MOBA_TOPK):
            m = jnp.max(work, axis=0, keepdims=True)
            first = jnp.min(jnp.where(work == m, brow, nb), axis=0, keepdims=True)
            pick = (brow == first) & (m > -jnp.inf)
            sel = jnp.where(pick, 1.0, sel)
            work = jnp.where(pick, -jnp.inf, work)
        bias_ref[hh] = jnp.where(sel > 0.0, 0.0, NEG_BIG)

        qb = (qh * (HEAD_DIM ** -0.5 * LOG2_E)).T.astype(BF16)
        s = jnp.dot(k_own, qb, preferred_element_type=F32)
        s = jnp.where(krow <= qcol, s, NEG_BIG)
        m0 = jnp.max(s, axis=0, keepdims=True)
        p = jnp.exp2(s - m0)
        acc0 = jnp.dot(vt_own, p.astype(BF16), preferred_element_type=F32)
        qbs.append(qb)
        state += [m0, jnp.sum(p, axis=0, keepdims=True), acc0]

    grp, ns = MOBA_BLOCKS_PER_STEP, MOBA_STREAMS
    sub = grp // ns
    for _ in range(1, ns):
        for hh in range(nh):
            state += [jnp.full((1, blk), NEG_BIG, F32), jnp.zeros((1, blk), F32), jnp.zeros((LANES, blk), F32)]

    def body(t, carry):
        out = []
        for sidx in range(ns):
            first = t * grp + sidx * sub
            rows = pl.ds(pl.multiple_of(first * blk, sub * blk), sub * blk)
            kj, vtj = kb_ref[rows, :], vt_ref[:, rows]
            for hh in range(nh):
                m_prev, l_prev, acc = carry[3 * (sidx * nh + hh):3 * (sidx * nh + hh) + 3]
                s = jnp.dot(kj, qbs[hh], preferred_element_type=F32)
                s = jnp.concatenate([s[g * blk:(g + 1) * blk] + bias_ref[hh, pl.ds(first + g, 1), :]
                                     for g in range(sub)], axis=0)
                m_new = jnp.maximum(m_prev, jnp.max(s, axis=0, keepdims=True))
                alpha = jnp.exp2(m_prev - m_new)
                p = jnp.exp2(s - m_new)
                out += [m_new, alpha * l_prev + jnp.sum(p, axis=0, keepdims=True),
                        alpha * acc + jnp.dot(vtj, p.astype(BF16), preferred_element_type=F32)]
        return tuple(out)

    fin = lax.fori_loop(0, (i + grp - 1) // grp, body, tuple(state))
    halves = []
    for hh in range(nh):
        parts = [fin[3 * (sidx * nh + hh):3 * (sidx * nh + hh) + 3] for sidx in range(ns)]
        m_fin = parts[0][0]
        for m_s, _, _ in parts[1:]:
            m_fin = jnp.maximum(m_fin, m_s)
        l_fin = sum(jnp.exp2(m_s - m_fin) * l_s for m_s, l_s, _ in parts)
        acc = sum(jnp.exp2(m_s - m_fin) * a_s for m_s, _, a_s in parts)
        halves.append(acc[hh * HEAD_DIM:(hh + 1) * HEAD_DIM] / l_fin)
    o_ref[0] = jnp.concatenate(halves, axis=0).T


def moba_attention(proj, *, q_col, k_col, v_col):
    bsz, seq, _ = proj.shape
    assert seq % (MOBA_BLOCK * MOBA_BLOCKS_PER_STEP) == 0
    nb = seq // MOBA_BLOCK
    qb, kb, vb = q_col // LANES, k_col // LANES, v_col // LANES
    return pl.pallas_call(
        _moba_kernel,
        grid=(bsz, MOBA_WIDTH // LANES, nb),
        in_specs=[
            pl.BlockSpec((1, MOBA_BLOCK, LANES), lambda b, p, i: (b, i, qb + p)),
            pl.BlockSpec((1, seq, LANES), lambda b, p, i: (b, 0, kb + p)),
            pl.BlockSpec((1, seq, LANES), lambda b, p, i: (b, 0, vb + p)),
        ],
        out_specs=pl.BlockSpec((1, MOBA_BLOCK, LANES), lambda b, p, i: (b, i, p)),
        out_shape=jax.ShapeDtypeStruct((bsz, seq, MOBA_WIDTH), F32),
        scratch_shapes=[pltpu.VMEM((nb, LANES), F32), pltpu.VMEM((seq, LANES), BF16),
                        pltpu.VMEM((LANES, seq), BF16),
                        pltpu.VMEM((LANES // HEAD_DIM, nb, MOBA_BLOCK), F32)],
        compiler_params=_cparams(("parallel", "parallel", "arbitrary")),
        name="moba_attention",
    )(proj, proj, proj)


def _head_sum_matrix(width):
    idx = np.arange(width) // HEAD_DIM
    return jnp.asarray((idx[:, None] == idx[None, :]).astype(np.float32)).astype(BF16)


def _rwkv_prep_kernel(p_ref, prev_ref, mu_ref, w0_ref, wup_ref, a0_ref, aup_ref, gup_ref, kk_ref, ka_ref,
                      rk_ref, hsum_ref, r_out, lw_out, k_out, v_out, kk_out, b_out, g_out, bonus_out):
    s = pl.program_id(1)
    p = p_ref[0]
    ts = p.shape[0]
    row = lax.broadcasted_iota(jnp.int32, p.shape, 0)
    last = jnp.where(s == 0, 0.0, 1.0) * prev_ref[0, SUBLANES - 1:SUBLANES, :]
    prev = jnp.where(row == 0, last, pltpu.roll(p, 1, axis=0))
    p = p + (prev - p) * mu_ref[...]
    w = RWKV_WIDTH
    r, k, v = p[:, 0:w], p[:, w:2 * w], p[:, 2 * w:3 * w]
    lora = p[:, 3 * w:3 * w + LANES]
    gd = p[:, 3 * w + LANES:3 * w + 2 * LANES]
    w_log = -jax.nn.softplus(-(w0_ref[...] + _mm3(jnp.tanh(lora), wup_ref[...]))) - 0.5
    lw_out[0] = -jnp.exp(w_log)
    a = jax.nn.sigmoid(a0_ref[...] + _mm3(lora, aup_ref[...]))
    g_out[0] = _mm(jax.nn.sigmoid(gd), gup_ref[...])
    kk = k * kk_ref[...]
    norm = jnp.sqrt(_mm_exact_rhs(kk * kk, hsum_ref[...]))
    kk = kk / jnp.maximum(norm, 1e-12)
    k = k * (1.0 + (a - 1.0) * ka_ref[...])
    r_out[0] = r
    k_out[0] = k
    v_out[0] = v
    kk_out[0] = kk
    b_out[0] = kk * a
    bonus_out[0] = _mm_exact_rhs(r * k * rk_ref[...], hsum_ref[...]) * v


RWKV_GROUP = 256


def _stack_heads(x):
    lane = lax.broadcasted_iota(jnp.int32, x.shape, 1)
    return jnp.concatenate([jnp.where(lane // HEAD_DIM == hh, x, 0.0)
                            for hh in range(RWKV_GROUP // HEAD_DIM)], axis=0)


def _rwkv_scan_kernel(r_ref, lw_ref, k_ref, v_ref, kk_ref, b_ref, g_ref, bonus_ref, lnw_ref, lnb_ref,
                      havg_ref, o_ref, y_ref, st_ref):
    s = pl.program_id(1)
    ts = r_ref.shape[1]
    cl_ = RWKV_CHUNK
    n = RWKV_GROUP
    ngroups = RWKV_WIDTH // RWKV_GROUP

    @pl.when(s == 0)
    def _():
        st_ref[...] = jnp.zeros_like(st_ref)

    ri = lax.broadcasted_iota(jnp.int32, (n, n), 0)
    ci = lax.broadcasted_iota(jnp.int32, (n, n), 1)
    same = (ri // cl_) == (ci // cl_)
    strict = same & ((ci % cl_) < (ri % cl_))
    incl = same & ((ci % cl_) <= (ri % cl_))
    eye = ri == ci
    tr = lax.broadcasted_iota(jnp.int32, (cl_, cl_), 0)
    tc = lax.broadcasted_iota(jnp.int32, (cl_, cl_), 1)
    cum = jnp.where(tc <= tr, 1.0, 0.0).astype(BF16)

    def chunk(c, _):
        rows = pl.ds(pl.multiple_of(c * cl_, cl_), cl_)
        lw = lw_ref[0, rows, :]
        lw_hi, lw_lo = _split(lw)
        cl_in = (jnp.dot(cum, lw_hi, preferred_element_type=F32)
                 + jnp.dot(cum, lw_lo, preferred_element_type=F32))
        cl_ex = cl_in - lw
        tot = cl_in[cl_ - 1:cl_, :]
        e_in = jnp.exp(cl_in)
        e_ninv = jnp.exp(-cl_in)
        e_rem = jnp.exp(tot - cl_in)
        kk = kk_ref[0, rows, :]
        bb = b_ref[0, rows, :]
        k2 = k_ref[0, rows, :]
        a_t = -kk * jnp.exp(cl_ex)
        b_t = bb * e_ninv
        k_t = k2 * e_ninv
        r_t = r_ref[0, rows, :] * e_in
        b_g = bb * e_rem
        k_g = k2 * e_rem
        vv = v_ref[0, rows, :]
        gam = jnp.exp(tot)
        for grp in range(ngroups):
            ls = slice(grp * n, (grp + 1) * n)
            a_s, b_s, k_s, r_s = (_stack_heads(t[:, ls]) for t in (a_t, b_t, k_t, r_t))
            v_s, bg_s, kg_s = (_stack_heads(t[:, ls]) for t in (vv, b_g, k_g))
            gram = _mm(jnp.concatenate([a_s, r_s], axis=0), jnp.concatenate([b_s, k_s], axis=0), _NT)
            nmat = jnp.where(strict, gram[:n, :n], 0.0)
            m_ak = jnp.where(strict, gram[:n, n:], 0.0)
            m_rb = jnp.where(incl, gram[n:, :n], 0.0)
            m_rk = jnp.where(incl, gram[n:, n:], 0.0)
            winv = jnp.where(eye, 1.0, 0.0) + nmat
            pw = nmat
            for _ in range(int(math.log2(cl_)) - 1):
                pw = _mm(pw, pw)
                winv = winv + _mm(winv, pw)
            st = st_ref[grp]
            ar_st = _mm(jnp.concatenate([a_s, r_s], axis=0), st)
            u_s = _mm(winv, ar_st[:n] + _mm(m_ak, v_s))
            y_s = ar_st[n:] + _mm(m_rb, u_s) + _mm(m_rk, v_s)
            y4 = y_s[0:cl_]
            for hh in range(1, n // cl_):
                y4 = y4 + y_s[hh * cl_:(hh + 1) * cl_]
            y_ref[rows, ls] = y4
            colg = jnp.sum(jnp.where(eye, jnp.broadcast_to(gam[:, ls], (n, n)), 0.0), axis=1, keepdims=True)
            st_ref[grp] = colg * st + _mm(jnp.concatenate([bg_s, kg_s], axis=0),
                                          jnp.concatenate([u_s, v_s], axis=0), _TN)
        return 0

    lax.fori_loop(0, ts // cl_, chunk, 0)

    y = y_ref[...]
    mean = _mm_exact_rhs(y, havg_ref[...])
    yc = y - mean
    var = _mm_exact_rhs(yc * yc, havg_ref[...])
    y = yc * lax.rsqrt(var + RWKV_GN_EPS) * lnw_ref[...] + lnb_ref[...]
    o_ref[0] = (y + bonus_ref[0]) * g_ref[0]


def rwkv7_time_mix(proj, mu, w0, w_up, a0, a_up, g_up, k_k, k_a, r_k, ln_w, ln_b, *, ts):
    bsz, seq, _ = proj.shape
    w = RWKV_WIDTH
    pw = 3 * w + 2 * LANES
    assert RWKV_W_LORA + RWKV_A_LORA == LANES and RWKV_G_LORA == LANES and seq % ts == 0
    wup = jnp.concatenate([w_up, jnp.zeros((RWKV_A_LORA, w), F32)], axis=0)
    aup = jnp.concatenate([jnp.zeros((RWKV_W_LORA, w), F32), a_up], axis=0)
    hsum = _head_sum_matrix(w)
    row = lambda t: t.reshape(1, -1)
    full = lambda shape: pl.BlockSpec(shape, lambda b, s: (0,) * len(shape))
    tile = pl.BlockSpec((1, ts, w), lambda b, s: (b, s, 0))
    nprev = ts // SUBLANES
    outs = pl.pallas_call(
        _rwkv_prep_kernel,
        grid=(bsz, seq // ts),
        in_specs=[
            pl.BlockSpec((1, ts, pw), lambda b, s: (b, s, 0)),
            pl.BlockSpec((1, SUBLANES, pw), lambda b, s: (b, jnp.maximum(s * nprev - 1, 0), 0)),
            full((1, pw)), full((1, w)), full((LANES, w)), full((1, w)), full((LANES, w)),
            full((LANES, w)), full((1, w)), full((1, w)), full((1, w)), full((w, w)),
        ],
        out_specs=[tile] * 8,
        out_shape=[jax.ShapeDtypeStruct((bsz, seq, w), F32)] * 8,
        compiler_params=_cparams(("parallel", "parallel")),
        name="rwkv_prep",
    )(proj, proj, row(mu), row(w0), wup, row(a0), aup, g_up.astype(BF16), row(k_k), row(k_a),
      row(r_k), hsum)
    havg = (_head_sum_matrix(w).astype(F32) / HEAD_DIM).astype(BF16)
    return pl.pallas_call(
        _rwkv_scan_kernel,
        grid=(bsz, seq // ts),
        in_specs=[tile] * 8 + [full((1, w)), full((1, w)), full((w, w))],
        out_specs=tile,
        out_shape=jax.ShapeDtypeStruct((bsz, seq, w), F32),
        scratch_shapes=[pltpu.VMEM((ts, w), F32),
                        pltpu.VMEM((w // RWKV_GROUP, RWKV_GROUP, RWKV_GROUP), F32)],
        compiler_params=_cparams(("parallel", "arbitrary")),
        name="rwkv_scan",
    )(*outs, row(ln_w), row(ln_b), havg)


DSA_KV_TILE = 512
DSA_SEARCH_TILE = 512
DSA_COUNT_ROWS = 64
DSA_STREAMS = 2
DSA_VROWS = 128
INT32_MIN = -2 ** 31
NEG_INF_KEY = -2139095041


def _sortable_key(x, pos, npos):
    bits = pltpu.bitcast(x, jnp.int32)
    key = jnp.where(bits < 0, bits ^ jnp.int32(0x7FFFFFFF), bits + npos)
    return jnp.where(x == 0.0, npos - 1 - pos, key)


def _dsa_kernel(q_ref, qi_ref, wi_ref, k_ref, v_ref, ki_ref, o_ref,
                kb_ref, vt_ref, kib_ref, key_ref, acc_ref):
    i = pl.program_id(1)
    nq, kt, st = DSA_QBLOCK, DSA_KV_TILE, DSA_SEARCH_TILE
    seq = k_ref.shape[1]
    topk = min(DSA_TOPK, seq // 4)

    @pl.when(i == 0)
    def _():
        kb_ref[...] = k_ref[0].astype(BF16)
        vrow = lax.broadcasted_iota(jnp.int32, (DSA_VROWS, seq), 0)
        ones_row = jnp.where(vrow == HEAD_DIM, 1.0, 0.0)
        vt_ref[...] = jnp.where(vrow < HEAD_DIM, v_ref[0].T[:DSA_VROWS], ones_row).astype(BF16)
        kib_ref[...] = ki_ref[0].astype(BF16)
        key_ref[...] = jnp.full(key_ref.shape, NEG_INF_KEY, jnp.int32)

    n_search = (i * nq) // st + 1
    spos = i * nq + lax.broadcasted_iota(jnp.int32, (st, nq), 1)
    lane = lax.broadcasted_iota(jnp.int32, (nq, LANES), 1)
    qpos = i * nq + lax.broadcasted_iota(jnp.int32, (kt, nq), 1)
    krow = lax.broadcasted_iota(jnp.int32, (kt, nq), 0)
    srow = lax.broadcasted_iota(jnp.int32, (st, nq), 0)

    def stack_heads(ref, nheads):
        parts = []
        for h in range(nheads):
            pair = ref[0, :, (h // 2) * LANES:(h // 2 + 1) * LANES]
            parts.append(jnp.where(lane // HEAD_DIM == h % 2, pair, 0.0))
        return jnp.concatenate(parts, axis=0)

    qib_t = stack_heads(qi_ref, DSA_IDX_HEADS).T.astype(BF16)
    wi_t = (wi_ref[0] * (DSA_IDX_HEADS ** -0.5 * HEAD_DIM ** -0.5)).T
    w_lanes = jnp.concatenate([wi_t[h:h + 1, :] for h in range(DSA_IDX_HEADS)], axis=1)

    def score_tile(c, _):
        rows = pl.ds(pl.multiple_of(c * kt, kt), kt)
        logit = jnp.dot(kib_ref[rows, :], qib_t, preferred_element_type=F32)
        weighted = jnp.maximum(logit, 0.0) * w_lanes
        score = weighted[:, 0:nq]
        for h in range(1, DSA_IDX_HEADS):
            score = score + weighted[:, h * nq:(h + 1) * nq]
        kpos = c * kt + krow
        score = jnp.where(kpos <= qpos, score, -jnp.inf)
        key_ref[rows, :] = _sortable_key(score, kpos, seq)
        return 0

    lax.fori_loop(0, n_search * (st // kt), score_tile, 0)

    def count(pred):
        def tile(c, acc):
            rows = pl.ds(pl.multiple_of(c * st, st), st)
            hit = jnp.where(pred(key_ref[rows, :], c * st + srow), 1.0, 0.0)
            for part in range(st // DSA_COUNT_ROWS):
                acc = acc + hit[part * DSA_COUNT_ROWS:(part + 1) * DSA_COUNT_ROWS]
            return acc
        acc = lax.fori_loop(0, n_search, tile, jnp.zeros((DSA_COUNT_ROWS, nq), F32))
        return jnp.sum(acc, axis=0, keepdims=True)

    def search_step(t, base):
        cand = base + lax.shift_left(jnp.int32(1), 31 - t)
        cnt = count(lambda key, pos: key >= cand)
        return jnp.where(cnt >= topk, cand, base)

    thr = lax.fori_loop(0, 32, search_step, jnp.full((1, nq), INT32_MIN, jnp.int32))
    need = topk - count(lambda key, pos: key > thr)
    n_eq = count(lambda key, pos: key == thr)
    tied = jnp.max(jnp.where((n_eq > need) & (thr > NEG_INF_KEY), 1.0, 0.0)) > 0.0
    nbits = seq.bit_length()

    def tie_search(_):
        def step(t, pos):
            cand = pos + lax.shift_left(jnp.int32(1), nbits - 1 - t)
            cnt = count(lambda key, kpos: (key == thr) & (kpos < cand))
            return jnp.where(cnt < need, cand, pos)
        return lax.fori_loop(0, nbits, step, jnp.zeros((1, nq), jnp.int32))

    last = lax.cond(tied, tie_search, lambda _: jnp.full((1, nq), seq, jnp.int32), 0)

    qs_t = (stack_heads(q_ref, DSA_HEADS) * (HEAD_DIM ** -0.5 * LOG2_E)).T.astype(BF16)
    acc_ref[...] = jnp.zeros_like(acc_ref)

    ns = DSA_STREAMS

    def attn_step(t, m_all):
        m_rows = []
        for sidx in range(ns):
            c = t * ns + sidx
            rows = pl.ds(pl.multiple_of(c * st, st), st)
            key = key_ref[rows, :]
            kpos = c * st + srow
            chosen = ((key > thr) | ((key == thr) & (kpos <= last))) & (kpos <= spos)
            bias = jnp.where(chosen, 0.0, NEG_BIG)
            kb, vt = kb_ref[rows, :], vt_ref[:, rows]
            for h in range(DSA_HEADS):
                slot = sidx * DSA_HEADS + h
                s = jnp.dot(kb, qs_t[:, h * nq:(h + 1) * nq], preferred_element_type=F32) + bias
                m_prev = m_all[slot:slot + 1, :]
                m_new = jnp.maximum(m_prev, jnp.max(s, axis=0, keepdims=True))
                p = jnp.exp2(s - m_new).astype(BF16)
                acc_ref[slot] = (jnp.exp2(m_prev - m_new) * acc_ref[slot]
                                 + jnp.dot(vt, p, preferred_element_type=F32))
                m_rows.append(m_new)
        return jnp.concatenate(m_rows, axis=0)

    m_all = lax.fori_loop(0, (n_search + ns - 1) // ns, attn_step,
                          jnp.full((ns * DSA_HEADS, nq), NEG_BIG, F32))

    def head_output(h):
        slots = [sidx * DSA_HEADS + h for sidx in range(ns)]
        m_fin = m_all[slots[0]:slots[0] + 1, :]
        for slot in slots[1:]:
            m_fin = jnp.maximum(m_fin, m_all[slot:slot + 1, :])
        acc = sum(jnp.exp2(m_all[slot:slot + 1, :] - m_fin) * acc_ref[slot] for slot in slots)
        return acc[:HEAD_DIM] / acc[HEAD_DIM:HEAD_DIM + 1]

    for pr in range(DSA_WIDTH // LANES):
        halves = [head_output(2 * pr), head_output(2 * pr + 1)]
        o_ref[0, :, pr * LANES:(pr + 1) * LANES] = jnp.concatenate(halves, axis=0).T


def dsa_attention(proj, *, q_col, qi_col, wi_col, k_col, v_col, ki_col):
    bsz, seq, _ = proj.shape
    assert seq % DSA_SEARCH_TILE == 0 and DSA_SEARCH_TILE % DSA_KV_TILE == 0 and DSA_KV_TILE % DSA_QBLOCK == 0
    nq = seq // DSA_QBLOCK
    kvspec = lambda col: pl.BlockSpec((1, seq, LANES), lambda b, i: (b, 0, col // LANES))
    return pl.pallas_call(
        _dsa_kernel,
        grid=(bsz, nq),
        in_specs=[
            pl.BlockSpec((1, DSA_QBLOCK, DSA_WIDTH), lambda b, i: (b, i, q_col // DSA_WIDTH)),
            pl.BlockSpec((1, DSA_QBLOCK, 2 * LANES), lambda b, i: (b, i, qi_col // (2 * LANES))),
            pl.BlockSpec((1, DSA_QBLOCK, LANES), lambda b, i: (b, i, wi_col // LANES)),
            kvspec(k_col), kvspec(v_col), kvspec(ki_col),
        ],
        out_specs=pl.BlockSpec((1, DSA_QBLOCK, DSA_WIDTH), lambda b, i: (b, i, 0)),
        out_shape=jax.ShapeDtypeStruct((bsz, seq, DSA_WIDTH), F32),
        scratch_shapes=[pltpu.VMEM((seq, LANES), BF16), pltpu.VMEM((DSA_VROWS, seq), BF16),
                        pltpu.VMEM((seq, LANES), BF16),
                        pltpu.VMEM((seq, DSA_QBLOCK), jnp.int32),
                        pltpu.VMEM((DSA_STREAMS * DSA_HEADS, DSA_VROWS, DSA_QBLOCK), F32)],
        compiler_params=_cparams(("parallel", "arbitrary")),
        name="dsa_attention",
    )(proj, proj, proj, proj, proj, proj)


MOE_TOKEN_TILE = 2048
MOE_ROW_BLOCK = 256
MOE_TAIL_BLOCK = 128
MOE_TOKEN_CHUNK = 512


def _router_kernel(res_ref, a_ref, b_ref, wa_ref, wb_ref, g_ref, wr_ref, h_ref, xn_ref, route_ref):
    acc = jnp.dot(a_ref[...].astype(BF16), wa_ref[...], preferred_element_type=F32)
    acc += jnp.dot(b_ref[...].astype(BF16), wb_ref[...], preferred_element_type=F32)
    h = res_ref[...] + acc
    h_ref[...] = h
    xn = _rms(h, g_ref[...])
    xn_ref[...] = xn.astype(BF16)
    lane = lax.broadcasted_iota(jnp.int32, (xn.shape[0], LANES), 1)
    logits = jnp.where(lane < N_EXPERTS, _mm3(xn, wr_ref[...]), -jnp.inf)
    v1 = jnp.max(logits, axis=1, keepdims=True)
    e1 = jnp.min(jnp.where(logits == v1, lane, LANES), axis=1, keepdims=True)
    rest = jnp.where(lane == e1, -jnp.inf, logits)
    v2 = jnp.max(rest, axis=1, keepdims=True)
    e2 = jnp.min(jnp.where(rest == v2, lane, LANES), axis=1, keepdims=True)
    ratio = jnp.exp(v2 - v1)
    g1 = 1.0 / (1.0 + ratio)
    g2 = ratio * g1
    route_ref[...] = jnp.where(lane == 0, e1.astype(F32), jnp.where(
        lane == 1, e2.astype(F32), jnp.where(lane == 2, g1, jnp.where(lane == 3, g2, 0.0))))


def out_proj_route_tokens(res, a, b, w_bf16, g, router, *, tm):
    t, d = res.shape
    ka, kb = a.shape[1], b.shape[1]
    wr = jnp.zeros((d, LANES), F32).at[:, :N_EXPERTS].set(router)
    rows = lambda width: pl.BlockSpec((tm, width), lambda i: (i, 0))
    whole = lambda r, c: pl.BlockSpec((r, c), lambda i: (0, 0))
    return pl.pallas_call(
        _router_kernel,
        grid=(t // tm,),
        in_specs=[rows(d), rows(ka), rows(kb), whole(ka, d), whole(kb, d), whole(1, d), whole(d, LANES)],
        out_specs=[rows(d), rows(d), rows(LANES)],
        out_shape=[jax.ShapeDtypeStruct((t, d), F32), jax.ShapeDtypeStruct((t, d), BF16),
                   jax.ShapeDtypeStruct((t, LANES), F32)],
        compiler_params=_cparams(("parallel",)),
        name="out_proj_router",
    )(res, a, b, w_bf16[:ka], w_bf16[ka:], g.reshape(1, d), wr)


def _moe_kernel(pc_ref, xn_ref, rk_ref, gt_ref, rkt_ref, wg_ref, wu_ref, wd_ref, o_ref, xs_ref, acc_ref):
    tau, e, f = pl.program_id(0), pl.program_id(1), pl.program_id(2)
    tt = xn_ref.shape[0]
    big, small, ch = MOE_ROW_BLOCK, MOE_TAIL_BLOCK, MOE_TOKEN_CHUNK
    nchunk = tt // ch
    pc_base = (tau * N_EXPERTS + e) * (nchunk + 1)

    def chunk_has_rows(c, first, rows):
        return (pc_ref[pc_base + c] < first + rows) & (pc_ref[pc_base + c + 1] > first)

    n_small = (pc_ref[pc_base + nchunk] + small - 1) // small
    n_big = n_small // (big // small)
    has_tail = n_small % (big // small) == 1

    def for_each_block(step):
        lax.fori_loop(0, n_big, lambda b, _: step(pl.multiple_of(b * big, big), big), 0)

        @pl.when(has_tail)
        def _():
            step(pl.multiple_of(n_big * big, big), small)

    @pl.when((e == 0) & (f == 0))
    def _():
        o_ref[...] = jnp.zeros_like(o_ref)

    @pl.when(f == 0)
    def _():
        rank = rk_ref[0, pl.ds(e, 1), :]

        def gather(first, rows):
            xs_ref[pl.ds(first, rows), :] = jnp.zeros((rows, xs_ref.shape[1]), BF16)
            acc_ref[pl.ds(first, rows), :] = jnp.zeros((rows, acc_ref.shape[1]), F32)
            slot = first + lax.broadcasted_iota(jnp.int32, (rows, ch), 0)
            for c in range(tt // ch):
                @pl.when(chunk_has_rows(c, first, rows))
                def _():
                    toks = slice(c * ch, (c + 1) * ch)
                    onehot = jnp.where(rank[:, toks] == slot, 1.0, 0.0).astype(BF16)
                    xs_ref[pl.ds(first, rows), :] += jnp.dot(
                        onehot, xn_ref[toks, :], preferred_element_type=F32).astype(BF16)
            return 0

        for_each_block(gather)

    def ffn(first, rows):
        xs = xs_ref[pl.ds(first, rows), :]
        gate = jnp.dot(xs, wg_ref[0], preferred_element_type=F32)
        up = jnp.dot(xs, wu_ref[0], preferred_element_type=F32)
        act = (gate * jax.nn.sigmoid(gate) * up).astype(BF16)
        acc_ref[pl.ds(first, rows), :] += jnp.dot(act, wd_ref[0], preferred_element_type=F32)
        return 0

    for_each_block(ffn)

    @pl.when(f == pl.num_programs(2) - 1)
    def _():
        rank = rk_ref[0, pl.ds(e, 1), :]
        gates = gt_ref[0, pl.ds(e, 1), :]
        lane_e = lax.broadcasted_iota(jnp.int32, rkt_ref.shape[1:], 1)
        rank_t = jnp.sum(jnp.where(lane_e == e, rkt_ref[0], 0), axis=1, keepdims=True)

        def combine(first, rows):
            slot = first + lax.broadcasted_iota(jnp.int32, (rows, tt), 0)
            gate_slot = jnp.sum(jnp.where(rank == slot, gates, 0.0), axis=1, keepdims=True)
            yg = (acc_ref[pl.ds(first, rows), :] * gate_slot).astype(BF16)
            slot_t = first + lax.broadcasted_iota(jnp.int32, (ch, rows), 1)
            for c in range(tt // ch):
                @pl.when(chunk_has_rows(c, first, rows))
                def _():
                    toks = slice(c * ch, (c + 1) * ch)
                    scatter = jnp.where(rank_t[toks, :] == slot_t, 1.0, 0.0).astype(BF16)
                    o_ref[toks, :] += jnp.dot(scatter, yg, preferred_element_type=F32)
            return 0

        for_each_block(combine)


def moe_experts(xn_bf16, route, w_gate, w_up, w_down, *, tf):
    t, d = xn_bf16.shape
    tt = MOE_TOKEN_TILE
    ntile = t // tt
    fdim = w_gate.shape[2]
    experts = route[:, 0:2].astype(jnp.int32).reshape(ntile, tt * 2)
    gates = route[:, 2:4].reshape(ntile, tt * 2)
    onehot = (experts[:, :, None] == jnp.arange(N_EXPERTS)[None, None, :]).astype(jnp.int32)
    csum = jnp.cumsum(onehot, axis=1)
    ends = csum[:, 2 * MOE_TOKEN_CHUNK - 1::2 * MOE_TOKEN_CHUNK, :]
    starts = jnp.concatenate([jnp.zeros((ntile, 1, N_EXPERTS), jnp.int32), ends], axis=1)
    chunk_starts = jnp.swapaxes(starts, 1, 2).reshape(-1)
    rank = jnp.where(onehot > 0, csum - 1, -1)
    rank_tok = jnp.max(rank.reshape(ntile, tt, 2, N_EXPERTS), axis=2)
    gate_tok = jnp.sum((onehot * gates[:, :, None]).reshape(ntile, tt, 2, N_EXPERTS), axis=2)
    rk = jnp.swapaxes(rank_tok, 1, 2)
    gt = jnp.swapaxes(gate_tok, 1, 2)
    grid_spec = pltpu.PrefetchScalarGridSpec(
        num_scalar_prefetch=1,
        grid=(ntile, N_EXPERTS, fdim // tf),
        in_specs=[
            pl.BlockSpec((tt, d), lambda i, e, f, c: (i, 0)),
            pl.BlockSpec((1, N_EXPERTS, tt), lambda i, e, f, c: (i, 0, 0)),
            pl.BlockSpec((1, N_EXPERTS, tt), lambda i, e, f, c: (i, 0, 0)),
            pl.BlockSpec((1, tt, N_EXPERTS), lambda i, e, f, c: (i, 0, 0)),
            pl.BlockSpec((1, d, tf), lambda i, e, f, c: (e, 0, f)),
            pl.BlockSpec((1, d, tf), lambda i, e, f, c: (e, 0, f)),
            pl.BlockSpec((1, tf, d), lambda i, e, f, c: (e, f, 0)),
        ],
        out_specs=pl.BlockSpec((tt, d), lambda i, e, f, c: (i, 0)),
        scratch_shapes=[pltpu.VMEM((tt, d), BF16), pltpu.VMEM((tt, d), F32)],
    )
    return pl.pallas_call(
        _moe_kernel,
        grid_spec=grid_spec,
        out_shape=jax.ShapeDtypeStruct((t, d), F32),
        compiler_params=_cparams(("parallel", "arbitrary", "arbitrary")),
        name="moe_experts",
    )(chunk_starts, xn_bf16, rk, gt, rank_tok, w_gate, w_up, w_down)


def _add_norm_kernel(a_ref, b_ref, g_ref, o_ref, *, normalize):
    y = a_ref[...] + b_ref[...]
    o_ref[...] = _rms(y, g_ref[...]) if normalize else y


def add_rmsnorm(a, b, g, *, tm):
    t, d = a.shape
    spec = pl.BlockSpec((tm, d), lambda i: (i, 0))
    gain = jnp.ones((1, d), F32) if g is None else g.reshape(1, d)
    return pl.pallas_call(
        functools.partial(_add_norm_kernel, normalize=g is not None),
        grid=(t // tm,),
        in_specs=[spec, spec, pl.BlockSpec((1, d), lambda i: (0, 0))],
        out_specs=spec,
        out_shape=jax.ShapeDtypeStruct((t, d), F32),
        compiler_params=_cparams(("parallel",)),
        name="add_rmsnorm",
    )(a, b, gain)


ROW_TILE = 1024
MOE_COL_TILE = 896
PROJ_COL_TILE = 1024
SEQ_TILE = 512
FFN_ROW_TILE = 1024
FFN_COL_TILE = 256

EVEN_Q_COL, EVEN_K_COL, EVEN_V_COL = 512, 1024, 1536
EVEN_ROPE_COLS = np.zeros(2048, bool)
EVEN_ROPE_COLS[EVEN_Q_COL:EVEN_V_COL] = True


def _odd_layout():
    rw = 3 * RWKV_WIDTH + RWKV_W_LORA + RWKV_A_LORA + RWKV_G_LORA
    q0 = rw
    k0 = q0 + DSA_WIDTH
    v0 = k0 + HEAD_DIM
    qi0 = v0 + HEAD_DIM
    ki0 = qi0 + DSA_IDX_HEADS * HEAD_DIM
    wi0 = ki0 + HEAD_DIM
    seg = lambda start, n: list(range(start, start + n))
    src, keep, rope = [], [], []

    def add(cols, roped, pad=0):
        src.extend(cols + [0] * pad)
        keep.extend([1.0] * len(cols) + [0.0] * pad)
        rope.extend([roped] * (len(cols) + pad))

    add(seg(0, rw), False)
    cols = {"k_col": len(src)}
    add(seg(k0, HEAD_DIM) * 2, True)
    cols["v_col"] = len(src)
    add(seg(v0, HEAD_DIM) * 2, False)
    cols["q_col"] = len(src)
    add(seg(q0, DSA_WIDTH), True)
    cols["qi_col"] = len(src)
    add(seg(qi0, DSA_IDX_HEADS * HEAD_DIM), True)
    cols["ki_col"] = len(src)
    add(seg(ki0, HEAD_DIM) * 2, True)
    cols["wi_col"] = len(src)
    add(seg(wi0, DSA_IDX_HEADS), False, pad=LANES - DSA_IDX_HEADS)
    return (np.asarray(src, np.int32), np.asarray(keep, np.float32), np.asarray(rope, bool), cols)


ODD_COL_SRC, ODD_COL_KEEP, ODD_ROPE_COLS, ODD_DSA_COLS = _odd_layout()


def kernel(x, e_norm_mix, e_w_in, s5_a_re, s5_a_im, s5_log_dt, s5_b_re, s5_b_im, s5_c_re, s5_c_im, s5_d, s5_glu_w, s5_glu_b, e_w_out, e_norm_ffn, ffn_w_gate, ffn_w_up, ffn_w_down, o_norm_mix, o_w_in, rwkv_mu, rwkv_w0, rwkv_w_up, rwkv_a0, rwkv_a_up, rwkv_g_up, rwkv_k_k, rwkv_k_a, rwkv_r_k, rwkv_ln_w, rwkv_ln_b, o_w_out, o_norm_ffn, moe_router, moe_w_gate, moe_w_up, moe_w_down, final_norm):
    bsz, seq, d = x.shape
    n_even, n_odd = e_norm_mix.shape[0], o_norm_mix.shape[0]
    h = x.reshape(-1, d)
    zero = jnp.zeros((bsz * seq, d), F32)
    for layer in range(n_even + n_odd):
        j = layer // 2
        last = layer == n_even + n_odd - 1
        if layer % 2 == 0:
            proj = norm_matmul_rope(h, e_norm_mix[j], e_w_in[j].astype(BF16), EVEN_ROPE_COLS, seq,
                                    tm=ROW_TILE, tn=PROJ_COL_TILE).reshape(bsz, seq, -1)
            ya = s5_mixer(proj, s5_a_re[j], s5_a_im[j], s5_log_dt[j], s5_b_re[j], s5_b_im[j], s5_c_re[j],
                          s5_c_im[j], s5_d[j], s5_glu_w[j], s5_glu_b[j], ts=SEQ_TILE)
            yb = moba_attention(proj, q_col=EVEN_Q_COL, k_col=EVEN_K_COL, v_col=EVEN_V_COL)
            h = out_proj_dense_ffn(h, ya.reshape(-1, S5_WIDTH), yb.reshape(-1, MOBA_WIDTH),
                                   e_w_out[j].astype(BF16), e_norm_ffn[j], ffn_w_gate[j].astype(BF16),
                                   ffn_w_up[j].astype(BF16), ffn_w_down[j].astype(BF16),
                                   tm=FFN_ROW_TILE, tf=FFN_COL_TILE)
            if last:
                h = add_rmsnorm(h, zero, final_norm, tm=ROW_TILE)
        else:
            w_in = jnp.take(o_w_in[j], jnp.asarray(ODD_COL_SRC), axis=1) * jnp.asarray(ODD_COL_KEEP)[None, :]
            proj = norm_matmul_rope(h, o_norm_mix[j], w_in.astype(BF16), ODD_ROPE_COLS, seq,
                                    tm=ROW_TILE, tn=PROJ_COL_TILE).reshape(bsz, seq, -1)
            yc = rwkv7_time_mix(proj, rwkv_mu[j], rwkv_w0[j], rwkv_w_up[j], rwkv_a0[j], rwkv_a_up[j],
                                rwkv_g_up[j], rwkv_k_k[j], rwkv_k_a[j], rwkv_r_k[j], rwkv_ln_w[j],
                                rwkv_ln_b[j], ts=SEQ_TILE)
            yd = dsa_attention(proj, **ODD_DSA_COLS)
            h, xn, route = out_proj_route_tokens(h, yc.reshape(-1, RWKV_WIDTH), yd.reshape(-1, DSA_WIDTH),
                                                 o_w_out[j].astype(BF16), o_norm_ffn[j], moe_router[j],
                                                 tm=ROW_TILE)
            y = moe_experts(xn, route, moe_w_gate[j].astype(BF16), moe_w_up[j].astype(BF16),
                            moe_w_down[j].astype(BF16), tf=MOE_COL_TILE)
            h = add_rmsnorm(h, y, final_norm if last else None, tm=ROW_TILE)
    return h.reshape(bsz, seq, d)
```
